```python
import math
import jax
import jax.numpy as jnp
from jax import lax
import numpy as np

D_MODEL = 1024
BATCH = 2
SEQ = 8192
DEPTH = 2
DEC_BATCH = 4
DEC_SEQ = 4096
PAST_LEN = 128

N_BRANCH = 4
MIX_W = D_MODEL // 4
N_HEADS = 4
FN_GROUP = MIX_W // N_HEADS
RET_DK = MIX_W // N_HEADS
RET_DV = MIX_W // N_HEADS
GLA_DK = MIX_W // N_HEADS // 2
GLA_DV = MIX_W // N_HEADS
GLA_RANK = 16
GLA_TAU = 16.0
HY_W = MIX_W
HY_ORDER = 2
FLT_BANDS = 16
FLT_EMB = 1 + 2 * FLT_BANDS
FLT_HIDDEN = 64
D_FF = 2816
CHUNK = 64
ROPE_BASE = 10000.0
EPS = 1e-6
IN_SIZES = (MIX_W,
            MIX_W, MIX_W, MIX_W, MIX_W,
            3 * HY_W,
            N_HEADS * GLA_DK, N_HEADS * GLA_DK,
            N_HEADS * GLA_DV, MIX_W,
            2 * GLA_RANK,
            N_BRANCH * D_MODEL)
N_IN = sum(IN_SIZES)

kernel_name = 'hybrid_bidir_encoder'


def rms_norm(x, w):
    xf = x.astype(jnp.float32)
    xf = xf * lax.rsqrt(jnp.mean(xf * xf, axis=-1, keepdims=True) + EPS)
    return xf.astype(x.dtype) * w


def conv3(x, w, b):
    xp = jnp.pad(x, ((0, 0), (1, 1), (0, 0)))
    return xp[:, :-2] * w[0] + xp[:, 1:-1] * w[1] + xp[:, 2:] * w[2] + b


def rotary(x):
    L, dh = x.shape[1], x.shape[-1]
    half = dh // 2
    inv = ROPE_BASE ** (-jnp.arange(half, dtype=jnp.float32) / half)
    ang = jnp.arange(L, dtype=jnp.float32)[:, None] * inv[None, :]
    cos = jnp.cos(ang)[None, :, None, :]
    sin = jnp.sin(ang)[None, :, None, :]
    x1, x2 = x[..., :half], x[..., half:]
    return jnp.concatenate([x1 * cos - x2 * sin, x1 * sin + x2 * cos], axis=-1)


def gated_linear_scan(q, k, v, g, strict):
    Bn, L, H, K = q.shape
    V = v.shape[-1]
    N = L // CHUNK
    q, k, g = (a.astype(jnp.float32).reshape(Bn, N, CHUNK, H, K) for a in (q, k, g))
    v = v.astype(jnp.float32).reshape(Bn, N, CHUNK, H, V)
    b = jnp.cumsum(g, axis=2)
    b_mid = b[:, :, CHUNK // 2:CHUNK // 2 + 1]
    b_last = b[:, :, -1:]
    scores = jnp.einsum('bnihk,bnjhk->bnhij', q * jnp.exp(b - b_mid), k * jnp.exp(b_mid - b))
    mask = jnp.tril(jnp.ones((CHUNK, CHUNK), dtype=bool), k=-1 if strict else 0)
    scores = jnp.where(mask, scores, 0.0)
    intra = jnp.einsum('bnhij,bnjhv->bnihv', scores, v)
    chunk_state = jnp.einsum('bnjhk,bnjhv->nbhkv', k * jnp.exp(b_last - b), v)
    chunk_decay = jnp.moveaxis(jnp.exp(b_last[:, :, 0]), 1, 0)

    def step(S, inp):
        dec, U = inp
        return dec[..., None] * S + U, S

    _, S_enter = lax.scan(step, jnp.zeros((Bn, H, K, V), jnp.float32), (chunk_decay, chunk_state))
    inter = jnp.einsum('bnihk,nbhkv->bnihv', q * jnp.exp(b), S_enter)
    return (intra + inter).reshape(Bn, L, H, V)


def bidirectional_gated_linear(q, k, v, g_fwd, g_bwd):
    fwd = gated_linear_scan(q, k, v, g_fwd, False)
    flip = lambda a: a[:, ::-1]
    bwd = flip(gated_linear_scan(flip(q), flip(k), flip(v), flip(g_bwd), True))
    return fwd + bwd


def head_norm(o, gain, center):
    Bn, L, H, dv = o.shape
    if center:
        o = o - jnp.mean(o, axis=-1, keepdims=True)
    o = o * lax.rsqrt(jnp.mean(o * o, axis=-1, keepdims=True) + EPS)
    return o.reshape(Bn, L, H * dv) * gain


def hyena_filters(L, flt_w1, flt_b1, flt_freq, flt_w2, flt_b2, flt_w3, flt_b3):
    t = jnp.linspace(0.0, 1.0, L, dtype=jnp.float32)[:, None]
    w = 2.0 * math.pi * jnp.arange(L, dtype=jnp.float32)[:, None] / L
    f = jnp.linspace(1e-4, FLT_BANDS - 1, FLT_BANDS, dtype=jnp.float32)[None, :]
    feat = jnp.concatenate([t, jnp.cos(f * w), -jnp.sin(f * w)], axis=-1)
    hdn = jnp.sin(flt_freq * (feat @ flt_w1 + flt_b1))
    hdn = jnp.sin(flt_freq * (hdn @ flt_w2 + flt_b2))
    filt = (hdn @ flt_w3 + flt_b3).astype(jnp.float32).reshape(L, HY_ORDER, 2, HY_W)
    deltas = jnp.abs(jnp.linspace(math.log(1e-2) / 0.3, math.log(1e-2) / 1.5, HY_W, dtype=jnp.float32))
    filt = filt * jnp.exp(-t[:, :, None, None] * deltas)
    hf, hb = filt[:, :, 0], filt[:, :, 1]
    g = jnp.concatenate([hf, jnp.zeros((1, HY_ORDER, HY_W), jnp.float32), hb[1:][::-1]], axis=0)
    g = g / (jnp.sum(jnp.abs(g), axis=0, keepdims=True) + EPS)
    return jnp.moveaxis(g, 1, 0)


def long_conv(u, g):
    L = u.shape[1]
    U = jnp.fft.rfft(u, n=2 * L, axis=1)
    G = jnp.fft.rfft(g, n=2 * L, axis=0)
    return jnp.fft.irfft(U * G[None], n=2 * L, axis=1)[:, :L]


def hyena_mixer(u, hy_conv_w, hy_conv_b, flt_w1, flt_b1, flt_freq, flt_w2, flt_b2, flt_w3, flt_b3, hy_skip):
    L = u.shape[1]
    u = conv3(u, hy_conv_w, hy_conv_b).astype(jnp.float32)
    v, x1, x2 = jnp.split(u, 3, axis=-1)
    filt = hyena_filters(L, flt_w1, flt_b1, flt_freq, flt_w2, flt_b2, flt_w3, flt_b3)
    z = x1 * (long_conv(v, filt[0]) + hy_skip[0] * v)
    z = x2 * (long_conv(z, filt[1]) + hy_skip[1] * z)
    return z


def token_mixer(h, w_in, hy_conv_w, hy_conv_b, flt_w1, flt_b1, flt_freq, flt_w2, flt_b2, flt_w3, flt_b3,
                hy_skip, gla_w_decay, gla_b_decay, ret_gn, gla_gn, w_branch, w_out):
    Bn, L, _ = h.shape
    proj = h @ w_in
    offsets = np.cumsum(IN_SIZES)[:-1].tolist()
    (u_fn, q_r, k_r, v_r, g_r, u_hy, q_g, k_g, v_g, g_g, lr_g, gates) = jnp.split(proj, offsets, axis=-1)
    heads = lambda a, d: a.astype(jnp.float32).reshape(Bn, L, N_HEADS, d)

    o_fn = jnp.fft.fftn(heads(u_fn, FN_GROUP), axes=(1, 3), norm='ortho').real.reshape(Bn, L, MIX_W)

    q = rotary(heads(q_r, RET_DK)) * RET_DK ** -0.5
    k = rotary(heads(k_r, RET_DK))
    v = heads(v_r, RET_DV)
    log_gamma = jnp.log(1.0 - 2.0 ** (-5.0 - jnp.arange(N_HEADS, dtype=jnp.float32)))
    g_ret = jnp.broadcast_to(log_gamma[None, None, :, None], q.shape)
    o_ret = bidirectional_gated_linear(q, k, v, g_ret, g_ret)
    o_ret = head_norm(o_ret, ret_gn, True) * jax.nn.silu(g_r)

    o_hy = hyena_mixer(u_hy, hy_conv_w, hy_conv_b, flt_w1, flt_b1, flt_freq, flt_w2, flt_b2, flt_w3, flt_b3, hy_skip)

    qg = heads(q_g, GLA_DK) * GLA_DK ** -0.5
    kg = heads(k_g, GLA_DK)
    vg = heads(v_g, GLA_DV)
    z = jnp.einsum('blrk,rkd->blrd', lr_g.astype(jnp.float32).reshape(Bn, L, 2, GLA_RANK), gla_w_decay) + gla_b_decay
    log_dec = (jax.nn.log_sigmoid(z.astype(jnp.float32)) / GLA_TAU).reshape(Bn, L, 2, N_HEADS, GLA_DK)
    o_gla = bidirectional_gated_linear(qg, kg, vg, log_dec[:, :, 0], log_dec[:, :, 1])
    o_gla = head_norm(o_gla, gla_gn, False) * jax.nn.silu(g_g)

    branches = jnp.stack([o_fn, o_ret, o_hy, o_gla], axis=2)
    proj_b = jnp.einsum('blnm,nmd->blnd', branches, w_branch)
    gate = jax.nn.sigmoid(gates.reshape(Bn, L, N_BRANCH, D_MODEL))
    merged = jnp.sum(gate * proj_b, axis=2)
    return merged @ w_out


def conv_ffn(h, ffn_up, ffn_conv_w, ffn_conv_b, ffn_down):
    a = conv3(h @ ffn_up, ffn_conv_w, ffn_conv_b)
    gate, val = jnp.split(a, 2, axis=-1)
    return (jax.nn.gelu(gate, approximate=True) * val) @ ffn_down


def encoder_layer(x, c, ada_w, ada_b, norm_pre_mix, norm_post_mix, norm_pre_ffn, norm_post_ffn, w_in,
                  hy_conv_w, hy_conv_b, flt_w1, flt_b1, flt_freq, flt_w2, flt_b2, flt_w3, flt_b3, hy_skip,
                  gla_w_decay, gla_b_decay, ret_gn, gla_gn, w_branch, w_out, ffn_up, ffn_conv_w, ffn_conv_b, ffn_down):
    mod = jax.nn.silu(c) @ ada_w + ada_b
    sh_m, sc_m, gt_m, sh_f, sc_f, gt_f = jnp.split(mod[:, None, :], 6, axis=-1)
    h = rms_norm(x, norm_pre_mix) * (1.0 + sc_m) + sh_m
    y = token_mixer(h, w_in, hy_conv_w, hy_conv_b, flt_w1, flt_b1, flt_freq, flt_w2, flt_b2, flt_w3, flt_b3,
                    hy_skip, gla_w_decay, gla_b_decay, ret_gn, gla_gn, w_branch, w_out)
    x = x + gt_m * rms_norm(y, norm_post_mix)
    h = rms_norm(x, norm_pre_ffn) * (1.0 + sc_f) + sh_f
    y = conv_ffn(h, ffn_up, ffn_conv_w, ffn_conv_b, ffn_down)
    return x + gt_f * rms_norm(y, norm_post_ffn)


def setup_inputs(seed: int = 0) -> dict:
    key = jax.random.key(seed)
    ks = jax.random.split(key, 32)
    f32 = jnp.float32

    def nrm(k, shape, scale):
        return scale * jax.random.normal(k, shape, f32)

    def gain(k, shape):
        return 1.0 + 0.1 * jax.random.normal(k, shape, f32)

    return {
        'x_prompt': nrm(ks[0], (BATCH, SEQ, D_MODEL), 1.0),
        'x_sample': nrm(ks[1], (DEC_BATCH, DEC_SEQ, D_MODEL), 1.0),
        'c_prompt': nrm(ks[2], (BATCH, D_MODEL), 1.0),
        'c_sample': nrm(ks[3], (DEC_BATCH, D_MODEL), 1.0),
        'ada_w': nrm(ks[4], (DEPTH, D_MODEL, 6 * D_MODEL), 0.5 * D_MODEL ** -0.5),
        'ada_b': nrm(ks[5], (DEPTH, 6 * D_MODEL), 0.02),
        'norm_pre_mix': gain(ks[6], (DEPTH, D_MODEL)),
        'norm_post_mix': gain(ks[7], (DEPTH, D_MODEL)),
        'norm_pre_ffn': gain(ks[8], (DEPTH, D_MODEL)),
        'norm_post_ffn': gain(ks[9], (DEPTH, D_MODEL)),
        'w_in': nrm(ks[10], (DEPTH, D_MODEL, N_IN), D_MODEL ** -0.5),
        'hy_conv_w': nrm(ks[11], (DEPTH, 3, 3 * HY_W), 3 ** -0.5),
        'hy_conv_b': nrm(ks[12], (DEPTH, 3 * HY_W), 0.02),
        'flt_w1': nrm(ks[13], (DEPTH, FLT_EMB, FLT_HIDDEN), FLT_EMB ** -0.5),
        'flt_b1': nrm(ks[14], (DEPTH, FLT_HIDDEN), 0.02),
        'flt_freq': gain(ks[15], (DEPTH, FLT_HIDDEN)),
        'flt_w2': nrm(ks[16], (DEPTH, FLT_HIDDEN, FLT_HIDDEN), FLT_HIDDEN ** -0.5),
        'flt_b2': nrm(ks[17], (DEPTH, FLT_HIDDEN), 0.02),
        'flt_w3': nrm(ks[18], (DEPTH, FLT_HIDDEN, HY_ORDER * 2 * HY_W), FLT_HIDDEN ** -0.5),
        'flt_b3': nrm(ks[19], (DEPTH, HY_ORDER * 2 * HY_W), 0.02),
        'hy_skip': nrm(ks[20], (DEPTH, HY_ORDER, HY_W), 0.5),
        'gla_w_decay': nrm(ks[21], (DEPTH, 2, GLA_RANK, N_HEADS * GLA_DK), GLA_RANK ** -0.5),
        'gla_b_decay': nrm(ks[22], (DEPTH, 2, N_HEADS * GLA_DK), 0.1),
        'ret_gn': gain(ks[23], (DEPTH, N_HEADS * RET_DV)),
        'gla_gn': gain(ks[24], (DEPTH, N_HEADS * GLA_DV)),
        'w_branch': nrm(ks[25], (DEPTH, N_BRANCH, MIX_W, D_MODEL), MIX_W ** -0.5),
        'w_out': nrm(ks[26], (DEPTH, D_MODEL, D_MODEL), D_MODEL ** -0.5),
        'ffn_up': nrm(ks[27], (DEPTH, D_MODEL, 2 * D_FF), D_MODEL ** -0.5),
        'ffn_conv_w': nrm(ks[28], (DEPTH, 3, 2 * D_FF), 3 ** -0.5),
        'ffn_conv_b': nrm(ks[29], (DEPTH, 2 * D_FF), 0.02),
        'ffn_down': nrm(ks[30], (DEPTH, D_FF, D_MODEL), D_FF ** -0.5),
    }


def reference(x_prompt, x_sample, c_prompt, c_sample, ada_w, ada_b, norm_pre_mix, norm_post_mix, norm_pre_ffn,
              norm_post_ffn, w_in, hy_conv_w, hy_conv_b, flt_w1, flt_b1, flt_freq, flt_w2, flt_b2, flt_w3, flt_b3,
              hy_skip, gla_w_decay, gla_b_decay, ret_gn, gla_gn, w_branch, w_out, ffn_up, ffn_conv_w, ffn_conv_b,
              ffn_down):
    y_prompt = x_prompt
    y_sample = x_sample
    for i in range(DEPTH):
        lp = (ada_w[i], ada_b[i], norm_pre_mix[i], norm_post_mix[i], norm_pre_ffn[i], norm_post_ffn[i], w_in[i],
              hy_conv_w[i], hy_conv_b[i], flt_w1[i], flt_b1[i], flt_freq[i], flt_w2[i], flt_b2[i], flt_w3[i],
              flt_b3[i], hy_skip[i], gla_w_decay[i], gla_b_decay[i], ret_gn[i], gla_gn[i], w_branch[i], w_out[i],
              ffn_up[i], ffn_conv_w[i], ffn_conv_b[i], ffn_down[i])
        y_prompt = encoder_layer(y_prompt, c_prompt, *lp)
        y_sample = encoder_layer(y_sample, c_sample, *lp)
    return (y_prompt, y_sample)
```

```python
import functools
import math

import numpy as np
import jax
import jax.numpy as jnp
from jax import lax
from jax.experimental import pallas as pl
from jax.experimental.pallas import tpu as pltpu

F32 = jnp.float32
BF16 = jnp.bfloat16
HIGHEST = lax.Precision.HIGHEST

D_MODEL = 1024
DEPTH = 2
N_BRANCH = 4
MIX_W = 256
N_HEADS = 4
HEAD_V = MIX_W // N_HEADS
RET_DK = 64
GLA_DK = 32
GLA_RANK = 16
GLA_TAU = 16.0
HY_ORDER = 2
FLT_BANDS = 16
FLT_EMB = 1 + 2 * FLT_BANDS
FLT_EMB_PAD = 40
FLT_HIDDEN = 64
D_FF = 2816
GLA_CHUNK = 64
RET_CHUNK = 128
ROPE_BASE = 10000.0
EPS = 1e-6
N_MIX_IN = 2848
GLA_IN_W = 896

V7X_VMEM_BYTES = 64 * 1024 * 1024
VMEM_LIMIT = 52 * 1024 * 1024
SUBLANES = 8
LANES = 128
FFT_N2 = 128


def _cparams(n_axes):
    return pltpu.CompilerParams(dimension_semantics=("arbitrary",) * n_axes,
                                vmem_limit_bytes=VMEM_LIMIT)


def _const_spec(shape):
    zeros = (0,) * len(shape)
    return pl.BlockSpec(shape, lambda *_: zeros, pipeline_mode=pl.Buffered(1))


def _dot(a, b):
    return jnp.dot(a, b, preferred_element_type=F32)


def _dot_nt(a, b):
    return lax.dot_general(a, b, (((1,), (1,)), ((), ())), preferred_element_type=F32)


def _dot_tn(a, b):
    return lax.dot_general(a, b, (((0,), (0,)), ((), ())), preferred_element_type=F32)


def _rms(x):
    return x * lax.rsqrt(jnp.mean(x * x, axis=-1, keepdims=True) + EPS)


def _silu(x):
    return x * jax.nn.sigmoid(x)


def _mod_kernel(c_ref, w_ref, b_ref, o_ref):
    c = c_ref[...]
    o_ref[...] = jnp.dot(_silu(c), w_ref[...], precision=HIGHEST, preferred_element_type=F32) + b_ref[...]


def _modulation(c_rows, ada_w, ada_b):
    rows = c_rows.shape[0]
    n_out = ada_w.shape[1]
    tn = D_MODEL
    return pl.pallas_call(
        _mod_kernel,
        grid=(n_out // tn,),
        in_specs=[pl.BlockSpec((rows, D_MODEL), lambda j: (0, 0)),
                  pl.BlockSpec((D_MODEL, tn), lambda j: (0, j)),
                  pl.BlockSpec((1, tn), lambda j: (0, j))],
        out_specs=pl.BlockSpec((rows, tn), lambda j: (0, j)),
        out_shape=jax.ShapeDtypeStruct((rows, n_out), F32),
        compiler_params=_cparams(1),
        name="modulation",
    )(c_rows, ada_w, ada_b.reshape(1, n_out))


def _mod_norm(x, norm_w, scale, shift):
    return _rms(x) * norm_w * (1.0 + scale) + shift


_MIX_SPLITS = ((0, 256), (256, 1280), (1280, 2048), (2048, 2048 + GLA_IN_W))


def _inproj_kernel(x_ref, mod_ref, nw_ref, w_ref, ofn_ref, oret_ref, ohy_ref, ogla_ref):
    h = _mod_norm(x_ref[0], nw_ref[...], mod_ref[0, 1:2, :], mod_ref[0, 0:1, :]).astype(BF16)
    for o_ref, (lo, hi) in zip((ofn_ref, oret_ref, ohy_ref, ogla_ref), _MIX_SPLITS):
        o_ref[0] = _dot(h, w_ref[:, lo:hi])


def _inproj(x, mod, norm_w, w_mix, tm):
    bn, seq, _ = x.shape
    widths = [hi - lo for lo, hi in _MIX_SPLITS]
    return pl.pallas_call(
        _inproj_kernel,
        grid=(bn, seq // tm),
        in_specs=[pl.BlockSpec((1, tm, D_MODEL), lambda b, t: (b, t, 0)),
                  pl.BlockSpec((1, 6, D_MODEL), lambda b, t: (b, 0, 0)),
                  _const_spec((1, D_MODEL)),
                  _const_spec(w_mix.shape)],
        out_specs=[pl.BlockSpec((1, tm, w), lambda b, t: (b, t, 0)) for w in widths],
        out_shape=[jax.ShapeDtypeStruct((bn, seq, w), F32) for w in widths],
        compiler_params=_cparams(2),
        name="inproj",
    )(x, mod, norm_w.reshape(1, D_MODEL), w_mix)


def _halo_specs(tm, seq, width):
    per = tm // SUBLANES
    last = seq // SUBLANES - 1
    prev_spec = pl.BlockSpec((1, SUBLANES, width), lambda b, t: (b, jnp.maximum(t * per - 1, 0), 0))
    next_spec = pl.BlockSpec((1, SUBLANES, width), lambda b, t: (b, jnp.minimum((t + 1) * per, last), 0))
    return prev_spec, next_spec


def _hyconv_kernel(u_ref, up_ref, un_ref, w_ref, b_ref, v_ref, x1_ref, x2_ref):
    t = pl.program_id(1)
    u = u_ref[0]
    tm = u.shape[0]
    prev_row = jnp.where(t > 0, up_ref[0, SUBLANES - 1:SUBLANES, :], 0.0)
    next_row = jnp.where(t < pl.num_programs(1) - 1, un_ref[0, 0:1, :], 0.0)
    rows = lax.broadcasted_iota(jnp.int32, u.shape, 0)
    below = jnp.where(rows == 0, prev_row, pltpu.roll(u, 1, 0))
    above = jnp.where(rows == tm - 1, next_row, pltpu.roll(u, tm - 1, 0))
    y = below * w_ref[0:1, :] + u * w_ref[1:2, :] + above * w_ref[2:3, :] + b_ref[...]
    v_ref[0] = y[:, 0:MIX_W]
    x1_ref[0] = y[:, MIX_W:2 * MIX_W]
    x2_ref[0] = y[:, 2 * MIX_W:3 * MIX_W]


def _hyena_conv3(u, conv_w, conv_b, tm):
    bn, seq, width = u.shape
    prev_spec, next_spec = _halo_specs(tm, seq, width)
    out = jax.ShapeDtypeStruct((bn, seq, MIX_W), F32)
    return pl.pallas_call(
        _hyconv_kernel,
        grid=(bn, seq // tm),
        in_specs=[pl.BlockSpec((1, tm, width), lambda b, t: (b, t, 0)), prev_spec, next_spec,
                  _const_spec((3, width)), _const_spec((1, width))],
        out_specs=[pl.BlockSpec((1, tm, MIX_W), lambda b, t: (b, t, 0))] * 3,
        out_shape=[out, out, out],
        compiler_params=_cparams(2),
        name="hyena_conv3",
    )(u, u, u, conv_w, conv_b.reshape(1, width))


def _dft_outer(n_total, n1, k_rows):
    del n_total
    k = np.arange(n1)[:, None]
    n = np.arange(k_rows)[None, :]
    ang = 2.0 * np.pi * ((k * n) % n1) / n1
    return jnp.asarray(np.concatenate([np.cos(ang), -np.sin(ang)], axis=0), dtype=F32)


def _idft_outer(n1, out_rows):
    n = np.arange(out_rows)[:, None]
    k = np.arange(n1)[None, :]
    ang = 2.0 * np.pi * ((k * n) % n1) / n1
    return jnp.asarray(np.concatenate([np.cos(ang), -np.sin(ang)], axis=1), dtype=F32)


def _dft_inner_tables(n_total, n1):
    n2 = np.arange(FFT_N2)
    ang = 2.0 * np.pi * ((n2[:, None] * n2[None, :]) % FFT_N2) / FFT_N2
    k1 = np.arange(n1)[:, None]
    tw = 2.0 * np.pi * ((k1 * n2[None, :]) % n_total) / n_total
    as32 = lambda a: jnp.asarray(a, dtype=F32)
    return as32(np.cos(ang)), as32(np.sin(ang)), as32(np.cos(tw)), as32(np.sin(tw))


def _lmul_kernel(f_ref, x_ref, o_ref):
    o_ref[0] = _dot(f_ref[...].astype(BF16), x_ref[0].astype(BF16)).astype(o_ref.dtype)


def _fft_outer(x_view, fmat, tn):
    bn, k_rows, width = x_view.shape
    r_rows = fmat.shape[0]
    return pl.pallas_call(
        _lmul_kernel,
        grid=(bn, width // tn),
        in_specs=[_const_spec(fmat.shape),
                  pl.BlockSpec((1, k_rows, tn), lambda b, j: (b, 0, j))],
        out_specs=pl.BlockSpec((1, r_rows, tn), lambda b, j: (b, 0, j)),
        out_shape=jax.ShapeDtypeStruct((bn, r_rows, width), BF16),
        compiler_params=_cparams(2),
        name="fft_outer",
    )(fmat, x_view)


def _build_inner_mats(fc_ref, fs_ref, twc_ref, tws_ref, m_scr, mt_scr, tk1):
    fc = fc_ref[...]
    fs = fs_ref[...]
    for i in range(tk1):
        tc = twc_ref[i:i + 1, :]
        ts = tws_ref[i:i + 1, :]
        ar = fc * tc - fs * ts
        ai = -(fc * ts + fs * tc)
        m_scr[i] = jnp.concatenate(
            [jnp.concatenate([ar, -ai], axis=1), jnp.concatenate([ai, ar], axis=1)], axis=0).astype(BF16)
        if mt_scr is not None:
            art = ar.T
            ait = ai.T
            mt_scr[i] = jnp.concatenate(
                [jnp.concatenate([art, ait], axis=1), jnp.concatenate([-ait, art], axis=1)], axis=0).astype(BF16)


def _inner_forward(m_scr, y_ref, i):
    ycat = jnp.concatenate([y_ref[0, 0, i], y_ref[0, 1, i]], axis=0)
    z = _dot(m_scr[i], ycat)
    return z[:FFT_N2], z[FFT_N2:]


def _spec_mid_kernel(tk1, fc_ref, fs_ref, twc_ref, tws_ref, y_ref, o_ref, m_scr):
    _build_inner_mats(fc_ref, fs_ref, twc_ref, tws_ref, m_scr, None, tk1)
    for i in range(tk1):
        zr, zi = _inner_forward(m_scr, y_ref, i)
        o_ref[i, 0] = zr
        o_ref[i, 1] = zi


def _conv_mid_kernel(tk1, fc_ref, fs_ref, twc_ref, tws_ref, y_ref, g_ref, o_ref, m_scr, mt_scr):
    @pl.when(pl.program_id(1) == 0)
    def _():
        _build_inner_mats(fc_ref, fs_ref, twc_ref, tws_ref, m_scr, mt_scr, tk1)

    for i in range(tk1):
        zr, zi = _inner_forward(m_scr, y_ref, i)
        gr = g_ref[i, 0]
        gi = g_ref[i, 1]
        pcat = jnp.concatenate([zr * gr - zi * gi, zr * gi + zi * gr], axis=0).astype(BF16)
        q = _dot(mt_scr[i], pcat)
        o_ref[0, 0, i] = q[:FFT_N2].astype(BF16)
        o_ref[0, 1, i] = q[FFT_N2:].astype(BF16)


def _fnet_mid_kernel(tk1, scale, fc_ref, fs_ref, twc_ref, tws_ref, y_ref, cs_ref, o_ref, m_scr):
    @pl.when(pl.program_id(1) == 0)
    def _():
        _build_inner_mats(fc_ref, fs_ref, twc_ref, tws_ref, m_scr, None, tk1)

    for i in range(tk1):
        zr, zi = _inner_forward(m_scr, y_ref, i)
        zcat = jnp.concatenate([zr, zi], axis=1).astype(BF16)
        o_ref[0, :, i * MIX_W:(i + 1) * MIX_W] = _dot(zcat, cs_ref[...].astype(BF16)) * scale


def _inner_table_specs(tk1):
    return [_const_spec((FFT_N2, FFT_N2)), _const_spec((FFT_N2, FFT_N2)),
            pl.BlockSpec((tk1, FFT_N2), lambda k, b: (k, 0)),
            pl.BlockSpec((tk1, FFT_N2), lambda k, b: (k, 0))]


FFT_TK1 = 8


def _filter_spectrum(g_raw, seq):
    n_total, chans = g_raw.shape
    n1 = n_total // FFT_N2
    y = _fft_outer(g_raw.reshape(1, n1, FFT_N2 * chans), _dft_outer(n_total, n1, n1), 4096)
    y = y.reshape(1, 2, n1, FFT_N2, chans)
    tables = _dft_inner_tables(n_total, n1)
    tk1 = FFT_TK1
    return pl.pallas_call(
        functools.partial(_spec_mid_kernel, tk1),
        grid=(n1 // tk1, 1),
        in_specs=_inner_table_specs(tk1) + [
            pl.BlockSpec((1, 2, tk1, FFT_N2, chans), lambda k, b: (0, 0, k, 0, 0))],
        out_specs=pl.BlockSpec((tk1, 2, FFT_N2, chans), lambda k, b: (k, 0, 0, 0)),
        out_shape=jax.ShapeDtypeStruct((n1, 2, FFT_N2, chans), F32),
        scratch_shapes=[pltpu.VMEM((tk1, 2 * FFT_N2, 2 * FFT_N2), BF16)],
        compiler_params=_cparams(2),
        name="filter_spectrum",
    )(*tables, y)


def _hy_out_kernel(n_total, f_ref, q_ref, v_ref, x_ref, gs_ref, sk_ref, o_ref):
    y = _dot(f_ref[...].astype(BF16), q_ref[0])
    inv = 1.0 / ((gs_ref[...] + EPS) * n_total)
    o_ref[0] = x_ref[0] * (y * inv + sk_ref[...] * v_ref[0])


def _long_conv_gate(v, x_gate, spectrum, order, gsum, skip, tn):
    bn, seq, chans = v.shape
    n_total = 2 * seq
    n1 = n_total // FFT_N2
    half = n1 // 2
    width = FFT_N2 * chans
    v_view = v.reshape(bn, half, width)
    y = _fft_outer(v_view, _dft_outer(n_total, n1, half), tn).reshape(bn, 2, n1, FFT_N2, chans)
    tables = _dft_inner_tables(n_total, n1)
    tk1 = FFT_TK1
    q = pl.pallas_call(
        functools.partial(_conv_mid_kernel, tk1),
        grid=(n1 // tk1, bn),
        in_specs=_inner_table_specs(tk1) + [
            pl.BlockSpec((1, 2, tk1, FFT_N2, chans), lambda k, b: (b, 0, k, 0, 0)),
            pl.BlockSpec((tk1, 2, FFT_N2, chans), lambda k, b: (k, 0, 0, order))],
        out_specs=pl.BlockSpec((1, 2, tk1, FFT_N2, chans), lambda k, b: (b, 0, k, 0, 0)),
        out_shape=jax.ShapeDtypeStruct((bn, 2, n1, FFT_N2, chans), BF16),
        scratch_shapes=[pltpu.VMEM((tk1, 2 * FFT_N2, 2 * FFT_N2), BF16),
                        pltpu.VMEM((tk1, 2 * FFT_N2, 2 * FFT_N2), BF16)],
        compiler_params=_cparams(2),
        name="long_conv_mid",
    )(*tables, y, spectrum)
    q_view = q.reshape(bn, 2 * n1, width)
    reps = tn // chans
    gs_row = jnp.tile(gsum.reshape(1, chans), (1, reps))
    sk_row = jnp.tile(skip.reshape(1, chans), (1, reps))
    fmat = _idft_outer(n1, half)
    out = pl.pallas_call(
        functools.partial(_hy_out_kernel, float(n_total)),
        grid=(bn, width // tn),
        in_specs=[_const_spec(fmat.shape),
                  pl.BlockSpec((1, 2 * n1, tn), lambda b, j: (b, 0, j)),
                  pl.BlockSpec((1, half, tn), lambda b, j: (b, 0, j)),
                  pl.BlockSpec((1, half, tn), lambda b, j: (b, 0, j)),
                  _const_spec((1, tn)), _const_spec((1, tn))],
        out_specs=pl.BlockSpec((1, half, tn), lambda b, j: (b, 0, j)),
        out_shape=jax.ShapeDtypeStruct((bn, half, width), F32),
        compiler_params=_cparams(2),
        name="long_conv_out",
    )(fmat, q_view, v_view, x_gate.reshape(bn, half, width), gs_row, sk_row)
    return out.reshape(bn, seq, chans)


def _filter_kernel(seq, feat_ref, w1_ref, b1_ref, fq_ref, w2_ref, b2_ref, w3_ref, b3_ref, dl_ref,
                   g_ref, s_ref):
    i = pl.program_id(0)
    f = feat_ref[...]
    tm = f.shape[0]
    fq = fq_ref[...]
    hp = functools.partial(jnp.dot, precision=HIGHEST, preferred_element_type=F32)
    h = jnp.sin(fq * (hp(f, w1_ref[...]) + b1_ref[...]))
    h = jnp.sin(fq * (hp(h, w2_ref[...]) + b2_ref[...]))
    filt = hp(h, w3_ref[...]) + b3_ref[...]
    win = jnp.exp(-f[:, 0:1] * dl_ref[...])
    r = i * tm + lax.broadcasted_iota(jnp.int32, (tm, MIX_W), 0)
    sums = []
    for o in range(HY_ORDER):
        hf = filt[:, (2 * o) * MIX_W:(2 * o + 1) * MIX_W]
        hb = filt[:, (2 * o + 1) * MIX_W:(2 * o + 2) * MIX_W]
        g = jnp.where(r < seq, hf, jnp.where(r > seq, hb, 0.0)) * win
        g_ref[:, o * MIX_W:(o + 1) * MIX_W] = g
        sums.append(jnp.sum(jnp.abs(g), axis=0, keepdims=True))
    total = jnp.concatenate(sums, axis=1)

    @pl.when(i == 0)
    def _():
        s_ref[...] = total

    @pl.when(i > 0)
    def _():
        s_ref[...] = s_ref[...] + total


def _hyena_filter_features(seq):
    t = jnp.linspace(0.0, 1.0, seq, dtype=F32)[:, None]
    w = 2.0 * math.pi * jnp.arange(seq, dtype=F32)[:, None] / seq
    f = jnp.linspace(1e-4, FLT_BANDS - 1, FLT_BANDS, dtype=F32)[None, :]
    feat = jnp.concatenate([t, jnp.cos(f * w), -jnp.sin(f * w)], axis=-1)
    feat = jnp.concatenate([feat, jnp.zeros((1, FLT_EMB), F32), feat[1:][::-1]], axis=0)
    return jnp.pad(feat, ((0, 0), (0, FLT_EMB_PAD - FLT_EMB)))


def _hyena_filters(seq, flt_w1, flt_b1, flt_freq, flt_w2, flt_b2, flt_w3, flt_b3):
    n_total = 2 * seq
    tm = 1024
    feat = _hyena_filter_features(seq)
    deltas = jnp.abs(jnp.linspace(math.log(1e-2) / 0.3, math.log(1e-2) / 1.5, MIX_W, dtype=F32))
    w1 = jnp.pad(flt_w1, ((0, FLT_EMB_PAD - FLT_EMB), (0, 0)))
    n_out = HY_ORDER * 2 * MIX_W
    g_raw, gsum = pl.pallas_call(
        functools.partial(_filter_kernel, seq),
        grid=(n_total // tm,),
        in_specs=[pl.BlockSpec((tm, FLT_EMB_PAD), lambda i: (i, 0)),
                  _const_spec((FLT_EMB_PAD, FLT_HIDDEN)), _const_spec((1, FLT_HIDDEN)),
                  _const_spec((1, FLT_HIDDEN)),
                  _const_spec((FLT_HIDDEN, FLT_HIDDEN)), _const_spec((1, FLT_HIDDEN)),
                  _const_spec((FLT_HIDDEN, n_out)), _const_spec((1, n_out)),
                  _const_spec((1, MIX_W))],
        out_specs=[pl.BlockSpec((tm, HY_ORDER * MIX_W), lambda i: (i, 0)),
                   pl.BlockSpec((1, HY_ORDER * MIX_W), lambda i: (0, 0))],
        out_shape=[jax.ShapeDtypeStruct((n_total, HY_ORDER * MIX_W), F32),
                   jax.ShapeDtypeStruct((1, HY_ORDER * MIX_W), F32)],
        compiler_params=_cparams(1),
        name="hyena_filter",
    )(feat, w1, flt_b1.reshape(1, -1), flt_freq.reshape(1, -1), flt_w2, flt_b2.reshape(1, -1),
      flt_w3, flt_b3.reshape(1, -1), deltas.reshape(1, MIX_W))
    return _filter_spectrum(g_raw, seq), gsum.reshape(HY_ORDER, MIX_W)


def _fnet_channel_mats():
    c = np.arange(MIX_W)
    same = (c[:, None] // HEAD_V) == (c[None, :] // HEAD_V)
    ang = 2.0 * np.pi * (((c[:, None] % HEAD_V) * (c[None, :] % HEAD_V)) % HEAD_V) / HEAD_V
    cs = np.concatenate([np.where(same, np.cos(ang), 0.0), np.where(same, np.sin(ang), 0.0)], axis=0)
    return jnp.asarray(cs, dtype=F32)


def _fnet(u):
    bn, seq, chans = u.shape
    n1 = seq // FFT_N2
    width = FFT_N2 * chans
    y = _fft_outer(u.reshape(bn, n1, width), _dft_outer(seq, n1, n1), 4096)
    y = y.reshape(bn, 2, n1, FFT_N2, chans)
    tables = _dft_inner_tables(seq, n1)
    tk1 = FFT_TK1
    scale = 1.0 / math.sqrt(seq * HEAD_V)
    out = pl.pallas_call(
        functools.partial(_fnet_mid_kernel, tk1, scale),
        grid=(n1 // tk1, bn),
        in_specs=_inner_table_specs(tk1) + [
            pl.BlockSpec((1, 2, tk1, FFT_N2, chans), lambda k, b: (b, 0, k, 0, 0)),
            _const_spec((2 * chans, chans))],
        out_specs=pl.BlockSpec((1, FFT_N2, tk1 * chans), lambda k, b: (b, 0, k)),
        out_shape=jax.ShapeDtypeStruct((bn, FFT_N2, n1 * chans), F32),
        scratch_shapes=[pltpu.VMEM((tk1, 2 * FFT_N2, 2 * FFT_N2), BF16)],
        compiler_params=_cparams(2),
        name="fnet_mid",
    )(*tables, y, _fnet_channel_mats())
    return out.reshape(bn, seq, chans)


def _split3(x):
    a = x.astype(BF16)
    r = x - a.astype(F32)
    b = r.astype(BF16)
    c = (r - b.astype(F32)).astype(BF16)
    return a, b, c


def _group_mean_mat():
    c = np.arange(MIX_W)
    same = (c[:, None] // HEAD_V) == (c[None, :] // HEAD_V)
    return jnp.asarray(np.where(same, 1.0 / HEAD_V, 0.0), dtype=BF16)


def _group_mean(x, avg):
    hi = x.astype(BF16)
    lo = (x - hi.astype(F32)).astype(BF16)
    return _dot(hi, avg) + _dot(lo, avg)


def _scan_kernel(cfg, *refs):
    chunk, dk_tot, reverse, is_ret, final = cfg
    refs = list(refs)
    q_ref, k_ref, v_ref = refs[:3]
    pos = 3
    if is_ret:
        cos_ref, sin_ref, lg_ref = refs[pos:pos + 3]
        pos += 3
    else:
        lr_ref, wd_ref, bd_ref = refs[pos:pos + 3]
        pos += 3
    if final:
        ofwd_ref, gate_ref, gain_ref, avg_ref = refs[pos:pos + 4]
        pos += 4
    o_ref, st_ref = refs[pos], refs[pos + 1]

    @pl.when(pl.program_id(1) == 0)
    def _():
        st_ref[...] = jnp.zeros_like(st_ref)

    dk = dk_tot // N_HEADS
    tb = q_ref.shape[1]
    n_chunks = tb // chunk
    wide = N_HEADS * chunk

    def iota(shape, axis):
        return lax.broadcasted_iota(jnp.int32, shape, axis)

    head_k = (iota((wide, dk_tot), 0) // chunk) == (iota((wide, dk_tot), 1) // dk)
    head_v = (iota((wide, MIX_W), 0) // chunk) == (iota((wide, MIX_W), 1) // HEAD_V)
    head_s = (iota((MIX_W, dk_tot), 0) // HEAD_V) == (iota((MIX_W, dk_tot), 1) // dk)
    ri = iota((chunk, wide), 0)
    ci = iota((chunk, wide), 1) % chunk
    keep = (ci > ri) if reverse else (ci <= ri)
    if not is_ret:
        ti = iota((chunk, chunk), 0)
        tj = iota((chunk, chunk), 1)
        tri = jnp.where((tj >= ti) if reverse else (tj <= ti), 1.0, 0.0).astype(BF16)
    if is_ret:
        lane = iota((chunk, MIX_W), 1)
        first_half = (lane % RET_DK) < (RET_DK // 2)
        steps = iota((chunk, dk_tot), 0)
        steps = ((chunk - steps) if reverse else (steps + 1)).astype(F32)

    order = range(n_chunks - 1, -1, -1) if reverse else range(n_chunks)
    for c in order:
        rows = slice(c * chunk, (c + 1) * chunk)
        q = q_ref[0, rows, :]
        k = k_ref[0, rows, :]
        v = v_ref[0, rows, :]
        if is_ret:
            cos = cos_ref[rows, :]
            sin = sin_ref[rows, :]

            def rotary(x):
                swapped = jnp.where(first_half, pltpu.roll(x, MIX_W - RET_DK // 2, 1),
                                    pltpu.roll(x, RET_DK // 2, 1))
                return x * cos + swapped * sin

            q = rotary(q)
            k = rotary(k)
            lg = lg_ref[...]
            b = steps * lg
            edge = float(chunk) * lg
            b_mid = float(chunk // 2 + 1) * lg
        else:
            z = jnp.dot(lr_ref[0, rows, :], wd_ref[...], precision=HIGHEST,
                        preferred_element_type=F32) + bd_ref[...]
            g = (jnp.minimum(z, 0.0) - jnp.log1p(jnp.exp(-jnp.abs(z)))) * (1.0 / GLA_TAU)
            g3 = _split3(g)
            b = _dot(tri, g3[0]) + _dot(tri, g3[1]) + _dot(tri, g3[2])
            edge = b[0:1, :] if reverse else b[chunk - 1:chunk, :]
            b_mid = b[chunk // 2:chunk // 2 + 1, :]
        q = q * (dk ** -0.5)
        qs = (q * jnp.exp(b - b_mid)).astype(BF16)
        ks = k * jnp.exp(b_mid - b)
        qi = (q * jnp.exp(b)).astype(BF16)
        kd = (k * jnp.exp(edge - b)).astype(BF16)
        k_bd = jnp.where(head_k, jnp.concatenate([ks] * N_HEADS, axis=0), 0.0).astype(BF16)
        v_bd = jnp.where(head_v, jnp.concatenate([v] * N_HEADS, axis=0), 0.0).astype(BF16)
        scores = jnp.where(keep, _dot_nt(qs, k_bd), 0.0).astype(BF16)
        state = st_ref[...]
        o = _dot(scores, v_bd) + _dot_nt(qi, state.astype(BF16))
        st_ref[...] = state * jnp.exp(edge) + jnp.where(head_s, _dot_tn(v.astype(BF16), kd), 0.0)
        if final:
            o = o + ofwd_ref[0, rows, :]
            avg = avg_ref[...]
            if is_ret:
                o = o - _group_mean(o, avg)
            o = o * lax.rsqrt(_group_mean(o * o, avg) + EPS)
            o = o * gain_ref[...] * _silu(gate_ref[0, rows, :])
        o_ref[0, rows, :] = o


def _scan_call(src, cols, extra, extra_specs, seq_block, chunk, dk_tot, reverse, is_ret, final_args):
    bn, seq, _ = src.shape
    nb = seq // seq_block
    blk = (lambda t: nb - 1 - t) if reverse else (lambda t: t)
    row_spec = lambda w, col: pl.BlockSpec((1, seq_block, w), lambda b, t: (b, blk(t), col))
    in_specs = [row_spec(dk_tot, cols[0]), row_spec(dk_tot, cols[1]), row_spec(MIX_W, cols[2])]
    args = [src, src, src]
    for a, s in zip(extra, extra_specs):
        args.append(a)
        in_specs.append(s(blk) if callable(s) else s)
    final = final_args is not None
    if final:
        o_fwd, gate_col, gain = final_args
        args += [o_fwd, src, gain.reshape(1, MIX_W), _group_mean_mat()]
        in_specs += [row_spec(MIX_W, 0), row_spec(MIX_W, gate_col), _const_spec((1, MIX_W)),
                     _const_spec((MIX_W, MIX_W))]
    cfg = (chunk, dk_tot, reverse, is_ret, final)
    return pl.pallas_call(
        functools.partial(_scan_kernel, cfg),
        grid=(bn, nb),
        in_specs=in_specs,
        out_specs=pl.BlockSpec((1, seq_block, MIX_W), lambda b, t: (b, blk(t), 0)),
        out_shape=jax.ShapeDtypeStruct((bn, seq, MIX_W), F32),
        scratch_shapes=[pltpu.VMEM((MIX_W, dk_tot), F32)],
        compiler_params=_cparams(2),
        name=("ret" if is_ret else "gla") + ("_bwd" if reverse else "_fwd"),
    )(*args)


def _rotary_tables(seq):
    half = RET_DK // 2
    inv = ROPE_BASE ** (-jnp.arange(half, dtype=F32) / half)
    ang = jnp.arange(seq, dtype=F32)[:, None] * inv[None, :]
    cos = jnp.cos(ang)
    sin = jnp.sin(ang)
    cos_t = jnp.tile(jnp.concatenate([cos, cos], axis=1), (1, N_HEADS))
    sin_t = jnp.tile(jnp.concatenate([-sin, sin], axis=1), (1, N_HEADS))
    return cos_t, sin_t


def _retention(p_ret, ret_gn, seq_block):
    bn, seq, _ = p_ret.shape
    cos_t, sin_t = _rotary_tables(seq)
    log_gamma = jnp.log(1.0 - 2.0 ** (-5.0 - jnp.arange(N_HEADS, dtype=F32)))
    lg = jnp.repeat(log_gamma, RET_DK).reshape(1, MIX_W)
    extra = [cos_t, sin_t, lg]
    tab = lambda blk: pl.BlockSpec((seq_block, MIX_W), lambda b, t: (blk(t), 0))
    specs = [tab, tab, _const_spec((1, MIX_W))]
    common = dict(src=p_ret, cols=(0, 1, 2), extra=extra, extra_specs=specs, seq_block=seq_block,
                  chunk=RET_CHUNK, dk_tot=N_HEADS * RET_DK, is_ret=True)
    o_fwd = _scan_call(reverse=False, final_args=None, **common)
    return _scan_call(reverse=True, final_args=(o_fwd, 3, ret_gn), **common)


def _gla(p_gla, w_decay, b_decay, gla_gn, seq_block):
    dk_tot = N_HEADS * GLA_DK
    outs = []
    o_fwd = None
    for direction in range(2):
        wd = jnp.zeros((LANES, dk_tot), F32).at[direction * GLA_RANK:(direction + 1) * GLA_RANK].set(
            w_decay[direction])
        extra = [p_gla, wd, b_decay[direction].reshape(1, dk_tot)]
        lr_spec = lambda blk: pl.BlockSpec((1, seq_block, LANES), lambda b, t: (b, blk(t), 6))
        specs = [lr_spec, _const_spec((LANES, dk_tot)), _const_spec((1, dk_tot))]
        final_args = None if direction == 0 else (o_fwd, 2, gla_gn)
        out = _scan_call(src=p_gla, cols=(0, 1, 1), extra=extra, extra_specs=specs, seq_block=seq_block,
                         chunk=GLA_CHUNK, dk_tot=dk_tot, reverse=bool(direction), is_ret=False,
                         final_args=final_args)
        o_fwd = out
        outs.append(out)
    return outs[1]


def _merge_kernel(x_ref, mod_ref, npre_ref, npost_ref, wg_ref, wb_ref, wo_ref,
                  fn_ref, ret_ref, hy_ref, gla_ref, o_ref):
    x = x_ref[0]
    h = _mod_norm(x, npre_ref[...], mod_ref[0, 1:2, :], mod_ref[0, 0:1, :]).astype(BF16)
    merged = None
    for n, br in enumerate((fn_ref, ret_ref, hy_ref, gla_ref)):
        gate = jax.nn.sigmoid(_dot(h, wg_ref[:, n * D_MODEL:(n + 1) * D_MODEL]))
        term = gate * _dot(br[0].astype(BF16), wb_ref[n])
        merged = term if merged is None else merged + term
    y = _dot(merged.astype(BF16), wo_ref[...])
    o_ref[0] = x + mod_ref[0, 2:3, :] * (_rms(y) * npost_ref[...])


def _merge(x, mod, norm_pre, norm_post, w_gate, w_branch, w_out, branches, tm):
    bn, seq, _ = x.shape
    row = lambda w: pl.BlockSpec((1, tm, w), lambda b, t: (b, t, 0))
    return pl.pallas_call(
        _merge_kernel,
        grid=(bn, seq // tm),
        in_specs=[row(D_MODEL), pl.BlockSpec((1, 6, D_MODEL), lambda b, t: (b, 0, 0)),
                  _const_spec((1, D_MODEL)), _const_spec((1, D_MODEL)),
                  _const_spec(w_gate.shape), _const_spec(w_branch.shape), _const_spec(w_out.shape)]
                 + [row(MIX_W)] * N_BRANCH,
        out_specs=row(D_MODEL),
        out_shape=jax.ShapeDtypeStruct(x.shape, F32),
        compiler_params=_cparams(2),
        name="merge",
    )(x, mod, norm_pre.reshape(1, D_MODEL), norm_post.reshape(1, D_MODEL), w_gate, w_branch, w_out,
      *branches)


FFN_COLS = 256


def _ffn_kernel(x_ref, xp_ref, xn_ref, mod_ref, npre_ref, npost_ref, wu_ref, cw_ref, cb_ref, wd_ref, o_ref):
    t = pl.program_id(1)
    x = x_ref[0]
    tm = x.shape[0]
    ext = tm + 2 * SUBLANES
    xe = jnp.concatenate([xp_ref[0], x, xn_ref[0]], axis=0)
    h = _mod_norm(xe, npre_ref[...], mod_ref[0, 4:5, :], mod_ref[0, 3:4, :]).astype(BF16)
    rows = lax.broadcasted_iota(jnp.int32, (ext, FFN_COLS), 0)
    valid = jnp.logical_and(jnp.logical_or(rows >= SUBLANES, t > 0),
                            jnp.logical_or(rows < tm + SUBLANES, t < pl.num_programs(1) - 1))

    def conv_cols(lo):
        a = jnp.where(valid, _dot(h, wu_ref[:, lo:lo + FFN_COLS]), 0.0)
        below = pltpu.roll(a, 1, 0)[SUBLANES:SUBLANES + tm]
        above = pltpu.roll(a, ext - 1, 0)[SUBLANES:SUBLANES + tm]
        mid = a[SUBLANES:SUBLANES + tm]
        return (below * cw_ref[0:1, lo:lo + FFN_COLS] + mid * cw_ref[1:2, lo:lo + FFN_COLS]
                + above * cw_ref[2:3, lo:lo + FFN_COLS] + cb_ref[:, lo:lo + FFN_COLS])

    y = None
    for j in range(D_FF // FFN_COLS):
        gate = conv_cols(j * FFN_COLS)
        val = conv_cols(D_FF + j * FFN_COLS)
        act = (jax.nn.gelu(gate, approximate=True) * val).astype(BF16)
        part = _dot(act, wd_ref[j * FFN_COLS:(j + 1) * FFN_COLS, :])
        y = part if y is None else y + part
    o_ref[0] = x + mod_ref[0, 5:6, :] * (_rms(y) * npost_ref[...])


def _conv_ffn(x, mod, norm_pre, norm_post, ffn_up, conv_w, conv_b, ffn_down, tm):
    bn, seq, _ = x.shape
    prev_spec, next_spec = _halo_specs(tm, seq, D_MODEL)
    row = pl.BlockSpec((1, tm, D_MODEL), lambda b, t: (b, t, 0))
    return pl.pallas_call(
        _ffn_kernel,
        grid=(bn, seq // tm),
        in_specs=[row, prev_spec, next_spec, pl.BlockSpec((1, 6, D_MODEL), lambda b, t: (b, 0, 0)),
                  _const_spec((1, D_MODEL)), _const_spec((1, D_MODEL)),
                  _const_spec(ffn_up.shape), _const_spec(conv_w.shape), _const_spec((1, 2 * D_FF)),
                  _const_spec(ffn_down.shape)],
        out_specs=row,
        out_shape=jax.ShapeDtypeStruct(x.shape, F32),
        compiler_params=_cparams(2),
        name="conv_ffn",
    )(x, x, x, mod, norm_pre.reshape(1, D_MODEL), norm_post.reshape(1, D_MODEL), ffn_up, conv_w,
      conv_b.reshape(1, 2 * D_FF), ffn_down)


def _row_tile(seq):
    return min(512, seq)


def _encoder_layer(x, mod, lw, filters):
    seq = x.shape[1]
    tm = _row_tile(seq)
    p_fn, p_ret, p_hy, p_gla = _inproj(x, mod, lw["norm_pre_mix"], lw["w_mix"], tm)

    o_fn = _fnet(p_fn)
    o_ret = _retention(p_ret, lw["ret_gn"], tm)

    spectrum, gsum = filters
    v, x1, x2 = _hyena_conv3(p_hy, lw["hy_conv_w"], lw["hy_conv_b"], tm)
    tn = min(4096, FFT_N2 * MIX_W)
    z = _long_conv_gate(v, x1, spectrum, 0, gsum[0], lw["hy_skip"][0], tn)
    o_hy = _long_conv_gate(z, x2, spectrum, 1, gsum[1], lw["hy_skip"][1], tn)

    o_gla = _gla(p_gla, lw["gla_w_decay"], lw["gla_b_decay"], lw["gla_gn"], tm)

    x = _merge(x, mod, lw["norm_pre_mix"], lw["norm_post_mix"], lw["w_gate"], lw["w_branch"], lw["w_out"],
               (o_fn, o_ret, o_hy, o_gla), tm)
    return _conv_ffn(x, mod, lw["norm_pre_ffn"], lw["norm_post_ffn"], lw["ffn_up"], lw["ffn_conv_w"],
                     lw["ffn_conv_b"], lw["ffn_down"], tm)


def kernel(x_prompt, x_sample, c_prompt, c_sample, ada_w, ada_b, norm_pre_mix, norm_post_mix, norm_pre_ffn, norm_post_ffn, w_in, hy_conv_w, hy_conv_b, flt_w1, flt_b1, flt_freq, flt_w2, flt_b2, flt_w3, flt_b3, hy_skip, gla_w_decay, gla_b_decay, ret_gn, gla_gn, w_branch, w_out, ffn_up, ffn_conv_w, ffn_conv_b, ffn_down):
    groups = [x_prompt, x_sample]
    n_rows = [c_prompt.shape[0], c_sample.shape[0]]
    c_rows = jnp.concatenate([c_prompt, c_sample], axis=0)
    pad = (-c_rows.shape[0]) % SUBLANES
    c_rows = jnp.pad(c_rows, ((0, pad), (0, 0)))
    for i in range(DEPTH):
        lw = {
            "norm_pre_mix": norm_pre_mix[i], "norm_post_mix": norm_post_mix[i],
            "norm_pre_ffn": norm_pre_ffn[i], "norm_post_ffn": norm_post_ffn[i],
            "w_mix": jnp.pad(w_in[i][:, :N_MIX_IN], ((0, 0), (0, 2048 + GLA_IN_W - N_MIX_IN))).astype(BF16),
            "w_gate": w_in[i][:, N_MIX_IN:].astype(BF16),
            "hy_conv_w": hy_conv_w[i], "hy_conv_b": hy_conv_b[i], "hy_skip": hy_skip[i],
            "gla_w_decay": gla_w_decay[i], "gla_b_decay": gla_b_decay[i],
            "ret_gn": ret_gn[i], "gla_gn": gla_gn[i],
            "w_branch": w_branch[i].astype(BF16), "w_out": w_out[i].astype(BF16),
            "ffn_up": ffn_up[i].astype(BF16), "ffn_conv_w": ffn_conv_w[i], "ffn_conv_b": ffn_conv_b[i],
            "ffn_down": ffn_down[i].astype(BF16),
        }
        mod_all = _modulation(c_rows, ada_w[i], ada_b[i])
        filters = {}
        start = 0
        for gi, x in enumerate(groups):
            seq = x.shape[1]
            if seq not in filters:
                filters[seq] = _hyena_filters(seq, flt_w1[i], flt_b1[i], flt_freq[i], flt_w2[i], flt_b2[i],
                                              flt_w3[i], flt_b3[i])
            mod = mod_all[start:start + n_rows[gi]].reshape(n_rows[gi], 6, D_MODEL)
            start += n_rows[gi]
            groups[gi] = _encoder_layer(x, mod, lw, filters[seq])
    return tuple(groups)
```

```python
import functools
import math

import numpy as np
import jax
import jax.numpy as jnp
from jax import lax
from jax.experimental import pallas as pl
from jax.experimental.pallas import tpu as pltpu

F32 = jnp.float32
BF16 = jnp.bfloat16
HIGHEST = lax.Precision.HIGHEST

D_MODEL = 1024
DEPTH = 2
N_BRANCH = 4
MIX_W = 256
N_HEADS = 4
HEAD_V = MIX_W // N_HEADS
RET_DK = 64
GLA_DK = 32
GLA_RANK = 16
GLA_TAU = 16.0
HY_ORDER = 2
FLT_BANDS = 16
FLT_EMB = 1 + 2 * FLT_BANDS
FLT_EMB_PAD = 40
FLT_HIDDEN = 64
D_FF = 2816
GLA_CHUNK = 64
RET_CHUNK = 128
SCAN_MAX_BATCH_BLOCK = 4
SCAN_BLOCK_ROWS = 1024
ROPE_BASE = 10000.0
EPS = 1e-6
N_MIX_IN = 2848
GLA_IN_W = 896

V7X_VMEM_BYTES = 64 * 1024 * 1024
VMEM_LIMIT = 52 * 1024 * 1024
SUBLANES = 8
LANES = 128
FFT_N2 = 128


def _cparams(n_axes):
    return pltpu.CompilerParams(dimension_semantics=("arbitrary",) * n_axes,
                                vmem_limit_bytes=VMEM_LIMIT)


def _const_spec(shape):
    zeros = (0,) * len(shape)
    return pl.BlockSpec(shape, lambda *_: zeros, pipeline_mode=pl.Buffered(1))


def _dot(a, b):
    return jnp.dot(a, b, preferred_element_type=F32)


def _dot_nt(a, b):
    return lax.dot_general(a, b, (((1,), (1,)), ((), ())), preferred_element_type=F32)


def _dot_tn(a, b):
    return lax.dot_general(a, b, (((0,), (0,)), ((), ())), preferred_element_type=F32)


def _rms(x):
    return x * lax.rsqrt(jnp.mean(x * x, axis=-1, keepdims=True) + EPS)


def _silu(x):
    return x * jax.nn.sigmoid(x)


def _mod_kernel(c_ref, w_ref, b_ref, o_ref):
    c = c_ref[...]
    o_ref[...] = jnp.dot(_silu(c), w_ref[...], precision=HIGHEST, preferred_element_type=F32) + b_ref[...]


def _modulation(c_rows, ada_w, ada_b):
    rows = c_rows.shape[0]
    n_out = ada_w.shape[1]
    tn = D_MODEL
    return pl.pallas_call(
        _mod_kernel,
        grid=(n_out // tn,),
        in_specs=[pl.BlockSpec((rows, D_MODEL), lambda j: (0, 0)),
                  pl.BlockSpec((D_MODEL, tn), lambda j: (0, j)),
                  pl.BlockSpec((1, tn), lambda j: (0, j))],
        out_specs=pl.BlockSpec((rows, tn), lambda j: (0, j)),
        out_shape=jax.ShapeDtypeStruct((rows, n_out), F32),
        compiler_params=_cparams(1),
        name="modulation",
    )(c_rows, ada_w, ada_b.reshape(1, n_out))


def _mod_norm(x, norm_w, scale, shift):
    return _rms(x) * norm_w * (1.0 + scale) + shift


_MIX_SPLITS = ((0, 256), (256, 1280), (1280, 2048), (2048, 2048 + GLA_IN_W))


def _inproj_kernel(x_ref, mod_ref, nw_ref, w_ref, ofn_ref, oret_ref, ohy_ref, ogla_ref):
    h = _mod_norm(x_ref[0], nw_ref[...], mod_ref[0, 1:2, :], mod_ref[0, 0:1, :]).astype(BF16)
    for o_ref, (lo, hi) in zip((ofn_ref, oret_ref, ohy_ref, ogla_ref), _MIX_SPLITS):
        o_ref[0] = _dot(h, w_ref[:, lo:hi])


def _inproj(x, mod, norm_w, w_mix, tm):
    bn, seq, _ = x.shape
    widths = [hi - lo for lo, hi in _MIX_SPLITS]
    return pl.pallas_call(
        _inproj_kernel,
        grid=(bn, seq // tm),
        in_specs=[pl.BlockSpec((1, tm, D_MODEL), lambda b, t: (b, t, 0)),
                  pl.BlockSpec((1, 6, D_MODEL), lambda b, t: (b, 0, 0)),
                  _const_spec((1, D_MODEL)),
                  _const_spec(w_mix.shape)],
        out_specs=[pl.BlockSpec((1, tm, w), lambda b, t: (b, t, 0)) for w in widths],
        out_shape=[jax.ShapeDtypeStruct((bn, seq, w), F32) for w in widths],
        compiler_params=_cparams(2),
        name="inproj",
    )(x, mod, norm_w.reshape(1, D_MODEL), w_mix)


def _halo_specs(tm, seq, width):
    per = tm // SUBLANES
    last = seq // SUBLANES - 1
    prev_spec = pl.BlockSpec((1, SUBLANES, width), lambda b, t: (b, jnp.maximum(t * per - 1, 0), 0))
    next_spec = pl.BlockSpec((1, SUBLANES, width), lambda b, t: (b, jnp.minimum((t + 1) * per, last), 0))
    return prev_spec, next_spec


def _hyconv_kernel(u_ref, up_ref, un_ref, w_ref, b_ref, v_ref, x1_ref, x2_ref):
    t = pl.program_id(1)
    u = u_ref[0]
    tm = u.shape[0]
    prev_row = jnp.where(t > 0, up_ref[0, SUBLANES - 1:SUBLANES, :], 0.0)
    next_row = jnp.where(t < pl.num_programs(1) - 1, un_ref[0, 0:1, :], 0.0)
    rows = lax.broadcasted_iota(jnp.int32, u.shape, 0)
    below = jnp.where(rows == 0, prev_row, pltpu.roll(u, 1, 0))
    above = jnp.where(rows == tm - 1, next_row, pltpu.roll(u, tm - 1, 0))
    y = below * w_ref[0:1, :] + u * w_ref[1:2, :] + above * w_ref[2:3, :] + b_ref[...]
    v_ref[0] = y[:, 0:MIX_W]
    x1_ref[0] = y[:, MIX_W:2 * MIX_W]
    x2_ref[0] = y[:, 2 * MIX_W:3 * MIX_W]


def _hyena_conv3(u, conv_w, conv_b, tm):
    bn, seq, width = u.shape
    prev_spec, next_spec = _halo_specs(tm, seq, width)
    out = jax.ShapeDtypeStruct((bn, seq, MIX_W), F32)
    return pl.pallas_call(
        _hyconv_kernel,
        grid=(bn, seq // tm),
        in_specs=[pl.BlockSpec((1, tm, width), lambda b, t: (b, t, 0)), prev_spec, next_spec,
                  _const_spec((3, width)), _const_spec((1, width))],
        out_specs=[pl.BlockSpec((1, tm, MIX_W), lambda b, t: (b, t, 0))] * 3,
        out_shape=[out, out, out],
        compiler_params=_cparams(2),
        name="hyena_conv3",
    )(u, u, u, conv_w, conv_b.reshape(1, width))


def _dft_outer(n_total, n1, k_rows):
    del n_total
    k = np.arange(n1)[:, None]
    n = np.arange(k_rows)[None, :]
    ang = 2.0 * np.pi * ((k * n) % n1) / n1
    return jnp.asarray(np.concatenate([np.cos(ang), -np.sin(ang)], axis=0), dtype=F32)


def _idft_outer(n1, out_rows):
    n = np.arange(out_rows)[:, None]
    k = np.arange(n1)[None, :]
    ang = 2.0 * np.pi * ((k * n) % n1) / n1
    return jnp.asarray(np.concatenate([np.cos(ang), -np.sin(ang)], axis=1), dtype=F32)


def _dft_inner_tables(n_total, n1):
    n2 = np.arange(FFT_N2)
    ang = 2.0 * np.pi * ((n2[:, None] * n2[None, :]) % FFT_N2) / FFT_N2
    k1 = np.arange(n1)[:, None]
    tw = 2.0 * np.pi * ((k1 * n2[None, :]) % n_total) / n_total
    as32 = lambda a: jnp.asarray(a, dtype=F32)
    return as32(np.cos(ang)), as32(np.sin(ang)), as32(np.cos(tw)), as32(np.sin(tw))


def _lmul_kernel(f_ref, x_ref, o_ref):
    o_ref[0] = _dot(f_ref[...].astype(BF16), x_ref[0].astype(BF16)).astype(o_ref.dtype)


def _fft_outer(x_view, fmat, tn):
    bn, k_rows, width = x_view.shape
    r_rows = fmat.shape[0]
    return pl.pallas_call(
        _lmul_kernel,
        grid=(bn, width // tn),
        in_specs=[_const_spec(fmat.shape),
                  pl.BlockSpec((1, k_rows, tn), lambda b, j: (b, 0, j))],
        out_specs=pl.BlockSpec((1, r_rows, tn), lambda b, j: (b, 0, j)),
        out_shape=jax.ShapeDtypeStruct((bn, r_rows, width), BF16),
        compiler_params=_cparams(2),
        name="fft_outer",
    )(fmat, x_view)


def _build_inner_mats(fc_ref, fs_ref, twc_ref, tws_ref, m_scr, mt_scr, tk1):
    fc = fc_ref[...]
    fs = fs_ref[...]
    for i in range(tk1):
        tc = twc_ref[i:i + 1, :]
        ts = tws_ref[i:i + 1, :]
        ar = fc * tc - fs * ts
        ai = -(fc * ts + fs * tc)
        m_scr[i] = jnp.concatenate(
            [jnp.concatenate([ar, -ai], axis=1), jnp.concatenate([ai, ar], axis=1)], axis=0).astype(BF16)
        if mt_scr is not None:
            art = ar.T
            ait = ai.T
            mt_scr[i] = jnp.concatenate(
                [jnp.concatenate([art, ait], axis=1), jnp.concatenate([-ait, art], axis=1)], axis=0).astype(BF16)


def _inner_forward(m_scr, y_ref, i):
    ycat = jnp.concatenate([y_ref[0, 0, i], y_ref[0, 1, i]], axis=0)
    z = _dot(m_scr[i], ycat)
    return z[:FFT_N2], z[FFT_N2:]


def _spec_mid_kernel(tk1, fc_ref, fs_ref, twc_ref, tws_ref, y_ref, o_ref, m_scr):
    _build_inner_mats(fc_ref, fs_ref, twc_ref, tws_ref, m_scr, None, tk1)
    for i in range(tk1):
        zr, zi = _inner_forward(m_scr, y_ref, i)
        o_ref[i, 0] = zr
        o_ref[i, 1] = zi


def _conv_mid_kernel(tk1, fc_ref, fs_ref, twc_ref, tws_ref, y_ref, g_ref, o_ref, m_scr, mt_scr):
    @pl.when(pl.program_id(1) == 0)
    def _():
        _build_inner_mats(fc_ref, fs_ref, twc_ref, tws_ref, m_scr, mt_scr, tk1)

    for i in range(tk1):
        zr, zi = _inner_forward(m_scr, y_ref, i)
        gr = g_ref[i, 0]
        gi = g_ref[i, 1]
        pcat = jnp.concatenate([zr * gr - zi * gi, zr * gi + zi * gr], axis=0).astype(BF16)
        q = _dot(mt_scr[i], pcat)
        o_ref[0, 0, i] = q[:FFT_N2].astype(BF16)
        o_ref[0, 1, i] = q[FFT_N2:].astype(BF16)


def _fnet_mid_kernel(tk1, scale, fc_ref, fs_ref, twc_ref, tws_ref, y_ref, cs_ref, o_ref, m_scr):
    @pl.when(pl.program_id(1) == 0)
    def _():
        _build_inner_mats(fc_ref, fs_ref, twc_ref, tws_ref, m_scr, None, tk1)

    for i in range(tk1):
        zr, zi = _inner_forward(m_scr, y_ref, i)
        zcat = jnp.concatenate([zr, zi], axis=1).astype(BF16)
        o_ref[0, :, i * MIX_W:(i + 1) * MIX_W] = _dot(zcat, cs_ref[...].astype(BF16)) * scale


def _inner_table_specs(tk1):
    return [_const_spec((FFT_N2, FFT_N2)), _const_spec((FFT_N2, FFT_N2)),
            pl.BlockSpec((tk1, FFT_N2), lambda k, b: (k, 0)),
            pl.BlockSpec((tk1, FFT_N2), lambda k, b: (k, 0))]


FFT_TK1 = 8


def _filter_spectrum(g_raw, seq):
    n_total, chans = g_raw.shape
    n1 = n_total // FFT_N2
    y = _fft_outer(g_raw.reshape(1, n1, FFT_N2 * chans), _dft_outer(n_total, n1, n1), 4096)
    y = y.reshape(1, 2, n1, FFT_N2, chans)
    tables = _dft_inner_tables(n_total, n1)
    tk1 = FFT_TK1
    return pl.pallas_call(
        functools.partial(_spec_mid_kernel, tk1),
        grid=(n1 // tk1, 1),
        in_specs=_inner_table_specs(tk1) + [
            pl.BlockSpec((1, 2, tk1, FFT_N2, chans), lambda k, b: (0, 0, k, 0, 0))],
        out_specs=pl.BlockSpec((tk1, 2, FFT_N2, chans), lambda k, b: (k, 0, 0, 0)),
        out_shape=jax.ShapeDtypeStruct((n1, 2, FFT_N2, chans), F32),
        scratch_shapes=[pltpu.VMEM((tk1, 2 * FFT_N2, 2 * FFT_N2), BF16)],
        compiler_params=_cparams(2),
        name="filter_spectrum",
    )(*tables, y)


def _hy_out_kernel(n_total, f_ref, q_ref, v_ref, x_ref, gs_ref, sk_ref, o_ref):
    y = _dot(f_ref[...].astype(BF16), q_ref[0])
    inv = 1.0 / ((gs_ref[...] + EPS) * n_total)
    o_ref[0] = x_ref[0] * (y * inv + sk_ref[...] * v_ref[0])


def _long_conv_gate(v, x_gate, spectrum, order, gsum, skip, tn):
    bn, seq, chans = v.shape
    n_total = 2 * seq
    n1 = n_total // FFT_N2
    half = n1 // 2
    width = FFT_N2 * chans
    v_view = v.reshape(bn, half, width)
    y = _fft_outer(v_view, _dft_outer(n_total, n1, half), tn).reshape(bn, 2, n1, FFT_N2, chans)
    tables = _dft_inner_tables(n_total, n1)
    tk1 = FFT_TK1
    q = pl.pallas_call(
        functools.partial(_conv_mid_kernel, tk1),
        grid=(n1 // tk1, bn),
        in_specs=_inner_table_specs(tk1) + [
            pl.BlockSpec((1, 2, tk1, FFT_N2, chans), lambda k, b: (b, 0, k, 0, 0)),
            pl.BlockSpec((tk1, 2, FFT_N2, chans), lambda k, b: (k, 0, 0, order))],
        out_specs=pl.BlockSpec((1, 2, tk1, FFT_N2, chans), lambda k, b: (b, 0, k, 0, 0)),
        out_shape=jax.ShapeDtypeStruct((bn, 2, n1, FFT_N2, chans), BF16),
        scratch_shapes=[pltpu.VMEM((tk1, 2 * FFT_N2, 2 * FFT_N2), BF16),
                        pltpu.VMEM((tk1, 2 * FFT_N2, 2 * FFT_N2), BF16)],
        compiler_params=_cparams(2),
        name="long_conv_mid",
    )(*tables, y, spectrum)
    q_view = q.reshape(bn, 2 * n1, width)
    reps = tn // chans
    gs_row = jnp.tile(gsum.reshape(1, chans), (1, reps))
    sk_row = jnp.tile(skip.reshape(1, chans), (1, reps))
    fmat = _idft_outer(n1, half)
    out = pl.pallas_call(
        functools.partial(_hy_out_kernel, float(n_total)),
        grid=(bn, width // tn),
        in_specs=[_const_spec(fmat.shape),
                  pl.BlockSpec((1, 2 * n1, tn), lambda b, j: (b, 0, j)),
                  pl.BlockSpec((1, half, tn), lambda b, j: (b, 0, j)),
                  pl.BlockSpec((1, half, tn), lambda b, j: (b, 0, j)),
                  _const_spec((1, tn)), _const_spec((1, tn))],
        out_specs=pl.BlockSpec((1, half, tn), lambda b, j: (b, 0, j)),
        out_shape=jax.ShapeDtypeStruct((bn, half, width), F32),
        compiler_params=_cparams(2),
        name="long_conv_out",
    )(fmat, q_view, v_view, x_gate.reshape(bn, half, width), gs_row, sk_row)
    return out.reshape(bn, seq, chans)


def _filter_kernel(seq, feat_ref, w1_ref, b1_ref, fq_ref, w2_ref, b2_ref, w3_ref, b3_ref, dl_ref,
                   g_ref, s_ref):
    i = pl.program_id(0)
    f = feat_ref[...]
    tm = f.shape[0]
    fq = fq_ref[...]
    hp = functools.partial(jnp.dot, precision=HIGHEST, preferred_element_type=F32)
    h = jnp.sin(fq * (hp(f, w1_ref[...]) + b1_ref[...]))
    h = jnp.sin(fq * (hp(h, w2_ref[...]) + b2_ref[...]))
    filt = hp(h, w3_ref[...]) + b3_ref[...]
    win = jnp.exp(-f[:, 0:1] * dl_ref[...])
    r = i * tm + lax.broadcasted_iota(jnp.int32, (tm, MIX_W), 0)
    sums = []
    for o in range(HY_ORDER):
        hf = filt[:, (2 * o) * MIX_W:(2 * o + 1) * MIX_W]
        hb = filt[:, (2 * o + 1) * MIX_W:(2 * o + 2) * MIX_W]
        g = jnp.where(r < seq, hf, jnp.where(r > seq, hb, 0.0)) * win
        g_ref[:, o * MIX_W:(o + 1) * MIX_W] = g
        sums.append(jnp.sum(jnp.abs(g), axis=0, keepdims=True))
    total = jnp.concatenate(sums, axis=1)

    @pl.when(i == 0)
    def _():
        s_ref[...] = total

    @pl.when(i > 0)
    def _():
        s_ref[...] = s_ref[...] + total


def _hyena_filter_features(seq):
    t = jnp.linspace(0.0, 1.0, seq, dtype=F32)[:, None]
    w = 2.0 * math.pi * jnp.arange(seq, dtype=F32)[:, None] / seq
    f = jnp.linspace(1e-4, FLT_BANDS - 1, FLT_BANDS, dtype=F32)[None, :]
    feat = jnp.concatenate([t, jnp.cos(f * w), -jnp.sin(f * w)], axis=-1)
    feat = jnp.concatenate([feat, jnp.zeros((1, FLT_EMB), F32), feat[1:][::-1]], axis=0)
    return jnp.pad(feat, ((0, 0), (0, FLT_EMB_PAD - FLT_EMB)))


def _hyena_filters(seq, flt_w1, flt_b1, flt_freq, flt_w2, flt_b2, flt_w3, flt_b3):
    n_total = 2 * seq
    tm = 1024
    feat = _hyena_filter_features(seq)
    deltas = jnp.abs(jnp.linspace(math.log(1e-2) / 0.3, math.log(1e-2) / 1.5, MIX_W, dtype=F32))
    w1 = jnp.pad(flt_w1, ((0, FLT_EMB_PAD - FLT_EMB), (0, 0)))
    n_out = HY_ORDER * 2 * MIX_W
    g_raw, gsum = pl.pallas_call(
        functools.partial(_filter_kernel, seq),
        grid=(n_total // tm,),
        in_specs=[pl.BlockSpec((tm, FLT_EMB_PAD), lambda i: (i, 0)),
                  _const_spec((FLT_EMB_PAD, FLT_HIDDEN)), _const_spec((1, FLT_HIDDEN)),
                  _const_spec((1, FLT_HIDDEN)),
                  _const_spec((FLT_HIDDEN, FLT_HIDDEN)), _const_spec((1, FLT_HIDDEN)),
                  _const_spec((FLT_HIDDEN, n_out)), _const_spec((1, n_out)),
                  _const_spec((1, MIX_W))],
        out_specs=[pl.BlockSpec((tm, HY_ORDER * MIX_W), lambda i: (i, 0)),
                   pl.BlockSpec((1, HY_ORDER * MIX_W), lambda i: (0, 0))],
        out_shape=[jax.ShapeDtypeStruct((n_total, HY_ORDER * MIX_W), F32),
                   jax.ShapeDtypeStruct((1, HY_ORDER * MIX_W), F32)],
        compiler_params=_cparams(1),
        name="hyena_filter",
    )(feat, w1, flt_b1.reshape(1, -1), flt_freq.reshape(1, -1), flt_w2, flt_b2.reshape(1, -1),
      flt_w3, flt_b3.reshape(1, -1), deltas.reshape(1, MIX_W))
    return _filter_spectrum(g_raw, seq), gsum.reshape(HY_ORDER, MIX_W)


def _fnet_channel_mats():
    c = np.arange(MIX_W)
    same = (c[:, None] // HEAD_V) == (c[None, :] // HEAD_V)
    ang = 2.0 * np.pi * (((c[:, None] % HEAD_V) * (c[None, :] % HEAD_V)) % HEAD_V) / HEAD_V
    cs = np.concatenate([np.where(same, np.cos(ang), 0.0), np.where(same, np.sin(ang), 0.0)], axis=0)
    return jnp.asarray(cs, dtype=F32)


def _fnet(u):
    bn, seq, chans = u.shape
    n1 = seq // FFT_N2
    width = FFT_N2 * chans
    y = _fft_outer(u.reshape(bn, n1, width), _dft_outer(seq, n1, n1), 4096)
    y = y.reshape(bn, 2, n1, FFT_N2, chans)
    tables = _dft_inner_tables(seq, n1)
    tk1 = FFT_TK1
    scale = 1.0 / math.sqrt(seq * HEAD_V)
    out = pl.pallas_call(
        functools.partial(_fnet_mid_kernel, tk1, scale),
        grid=(n1 // tk1, bn),
        in_specs=_inner_table_specs(tk1) + [
            pl.BlockSpec((1, 2, tk1, FFT_N2, chans), lambda k, b: (b, 0, k, 0, 0)),
            _const_spec((2 * chans, chans))],
        out_specs=pl.BlockSpec((1, FFT_N2, tk1 * chans), lambda k, b: (b, 0, k)),
        out_shape=jax.ShapeDtypeStruct((bn, FFT_N2, n1 * chans), F32),
        scratch_shapes=[pltpu.VMEM((tk1, 2 * FFT_N2, 2 * FFT_N2), BF16)],
        compiler_params=_cparams(2),
        name="fnet_mid",
    )(*tables, y, _fnet_channel_mats())
    return out.reshape(bn, seq, chans)


def _split3(x):
    a = x.astype(BF16)
    r = x - a.astype(F32)
    b = r.astype(BF16)
    c = (r - b.astype(F32)).astype(BF16)
    return a, b, c


def _group_mean_mat():
    c = np.arange(MIX_W)
    same = (c[:, None] // HEAD_V) == (c[None, :] // HEAD_V)
    return jnp.asarray(np.where(same, 1.0 / HEAD_V, 0.0), dtype=BF16)


def _group_mean(x, avg):
    hi = x.astype(BF16)
    lo = (x - hi.astype(F32)).astype(BF16)
    return _dot(hi, avg) + _dot(lo, avg)


def _scan_kernel(cfg, *refs):
    chunk, dk_tot, reverse, is_ret, final = cfg
    refs = list(refs)
    q_ref, k_ref, v_ref = refs[:3]
    pos = 3
    if is_ret:
        cos_ref, sin_ref, lg_ref = refs[pos:pos + 3]
        pos += 3
    else:
        lr_ref, wd_ref, bd_ref = refs[pos:pos + 3]
        pos += 3
    if final:
        ofwd_ref, gate_ref, gain_ref, avg_ref = refs[pos:pos + 4]
        pos += 4
    o_ref, st_ref = refs[pos], refs[pos + 1]

    @pl.when(pl.program_id(1) == 0)
    def _():
        st_ref[...] = jnp.zeros_like(st_ref)

    dk = dk_tot // N_HEADS
    tb = q_ref.shape[1]
    n_chunks = tb // chunk
    wide = N_HEADS * chunk

    def iota(shape, axis):
        return lax.broadcasted_iota(jnp.int32, shape, axis)

    head_k = (iota((wide, dk_tot), 0) // chunk) == (iota((wide, dk_tot), 1) // dk)
    head_v = (iota((wide, MIX_W), 0) // chunk) == (iota((wide, MIX_W), 1) // HEAD_V)
    head_s = (iota((MIX_W, dk_tot), 0) // HEAD_V) == (iota((MIX_W, dk_tot), 1) // dk)
    ri = iota((chunk, wide), 0)
    ci = iota((chunk, wide), 1) % chunk
    keep = (ci > ri) if reverse else (ci <= ri)
    if not is_ret:
        ti = iota((chunk, chunk), 0)
        tj = iota((chunk, chunk), 1)
        tri = jnp.where((tj >= ti) if reverse else (tj <= ti), 1.0, 0.0).astype(BF16)
    if is_ret:
        lane = iota((chunk, MIX_W), 1)
        first_half = (lane % RET_DK) < (RET_DK // 2)
        steps = iota((chunk, dk_tot), 0)
        steps = ((chunk - steps) if reverse else (steps + 1)).astype(F32)

    if is_ret:
        lg = lg_ref[...]
        b_ret = steps * lg
        edge_ret = float(chunk) * lg
        b_mid_ret = float(chunk // 2 + 1) * lg

    def chunk_step(bi, c):
        rows = slice(c * chunk, (c + 1) * chunk)
        q = q_ref[bi, rows, :]
        k = k_ref[bi, rows, :]
        v = v_ref[bi, rows, :]
        if is_ret:
            cos = cos_ref[rows, :]
            sin = sin_ref[rows, :]

            def rotary(x):
                swapped = jnp.where(first_half, pltpu.roll(x, MIX_W - RET_DK // 2, 1),
                                    pltpu.roll(x, RET_DK // 2, 1))
                return x * cos + swapped * sin

            q = rotary(q)
            k = rotary(k)
            b, edge, b_mid = b_ret, edge_ret, b_mid_ret
        else:
            z = jnp.dot(lr_ref[bi, rows, :], wd_ref[...], precision=HIGHEST,
                        preferred_element_type=F32) + bd_ref[...]
            g = (jnp.minimum(z, 0.0) - jnp.log1p(jnp.exp(-jnp.abs(z)))) * (1.0 / GLA_TAU)
            g3 = _split3(g)
            b = _dot(tri, g3[0]) + _dot(tri, g3[1]) + _dot(tri, g3[2])
            edge = b[0:1, :] if reverse else b[chunk - 1:chunk, :]
            b_mid = b[chunk // 2:chunk // 2 + 1, :]
        q = q * (dk ** -0.5)
        qs = (q * jnp.exp(b - b_mid)).astype(BF16)
        ks = k * jnp.exp(b_mid - b)
        qi = (q * jnp.exp(b)).astype(BF16)
        kd = (k * jnp.exp(edge - b)).astype(BF16)
        k_bd = jnp.where(head_k, jnp.concatenate([ks] * N_HEADS, axis=0), 0.0).astype(BF16)
        v_bd = jnp.where(head_v, jnp.concatenate([v] * N_HEADS, axis=0), 0.0).astype(BF16)
        scores = jnp.where(keep, _dot_nt(qs, k_bd), 0.0).astype(BF16)
        state = st_ref[bi]
        o = _dot(scores, v_bd) + _dot_nt(qi, state.astype(BF16))
        st_ref[bi] = state * jnp.exp(edge) + jnp.where(head_s, _dot_tn(v.astype(BF16), kd), 0.0)
        if final:
            o = o + ofwd_ref[bi, rows, :]
            avg = avg_ref[...]
            if is_ret:
                o = o - _group_mean(o, avg)
            o = o * lax.rsqrt(_group_mean(o * o, avg) + EPS)
            o = o * gain_ref[...] * _silu(gate_ref[bi, rows, :])
        o_ref[bi, rows, :] = o

    for c in (range(n_chunks - 1, -1, -1) if reverse else range(n_chunks)):
        for bi in range(q_ref.shape[0]):
            chunk_step(bi, c)


def _scan_call(src, cols, extra, extra_specs, seq_block, chunk, dk_tot, reverse, is_ret, final_args):
    bn, seq, _ = src.shape
    bb = bn if bn <= SCAN_MAX_BATCH_BLOCK else SCAN_MAX_BATCH_BLOCK
    seq_block = min(seq, SCAN_BLOCK_ROWS // bb)
    nb = seq // seq_block
    blk = (lambda t: nb - 1 - t) if reverse else (lambda t: t)
    row_spec = lambda w, col: pl.BlockSpec((bb, seq_block, w), lambda b, t: (b, blk(t), col))
    in_specs = [row_spec(dk_tot, cols[0]), row_spec(dk_tot, cols[1]), row_spec(MIX_W, cols[2])]
    args = [src, src, src]
    for a, s in zip(extra, extra_specs):
        args.append(a)
        in_specs.append(s(blk, bb, seq_block) if callable(s) else s)
    final = final_args is not None
    if final:
        o_fwd, gate_col, gain = final_args
        args += [o_fwd, src, gain.reshape(1, MIX_W), _group_mean_mat()]
        in_specs += [row_spec(MIX_W, 0), row_spec(MIX_W, gate_col), _const_spec((1, MIX_W)),
                     _const_spec((MIX_W, MIX_W))]
    cfg = (chunk, dk_tot, reverse, is_ret, final)
    return pl.pallas_call(
        functools.partial(_scan_kernel, cfg),
        grid=(bn // bb, nb),
        in_specs=in_specs,
        out_specs=pl.BlockSpec((bb, seq_block, MIX_W), lambda b, t: (b, blk(t), 0)),
        out_shape=jax.ShapeDtypeStruct((bn, seq, MIX_W), F32),
        scratch_shapes=[pltpu.VMEM((bb, MIX_W, dk_tot), F32)],
        compiler_params=_cparams(2),
        name=("ret" if is_ret else "gla") + ("_bwd" if reverse else "_fwd"),
    )(*args)


def _rotary_tables(seq):
    half = RET_DK // 2
    inv = ROPE_BASE ** (-jnp.arange(half, dtype=F32) / half)
    ang = jnp.arange(seq, dtype=F32)[:, None] * inv[None, :]
    cos = jnp.cos(ang)
    sin = jnp.sin(ang)
    cos_t = jnp.tile(jnp.concatenate([cos, cos], axis=1), (1, N_HEADS))
    sin_t = jnp.tile(jnp.concatenate([-sin, sin], axis=1), (1, N_HEADS))
    return cos_t, sin_t


def _retention(p_ret, ret_gn, seq_block):
    bn, seq, _ = p_ret.shape
    cos_t, sin_t = _rotary_tables(seq)
    log_gamma = jnp.log(1.0 - 2.0 ** (-5.0 - jnp.arange(N_HEADS, dtype=F32)))
    lg = jnp.repeat(log_gamma, RET_DK).reshape(1, MIX_W)
    extra = [cos_t, sin_t, lg]
    tab = lambda blk, bb, rows: pl.BlockSpec((rows, MIX_W), lambda b, t: (blk(t), 0))
    specs = [tab, tab, _const_spec((1, MIX_W))]
    common = dict(src=p_ret, cols=(0, 1, 2), extra=extra, extra_specs=specs, seq_block=seq_block,
                  chunk=RET_CHUNK, dk_tot=N_HEADS * RET_DK, is_ret=True)
    o_fwd = _scan_call(reverse=False, final_args=None, **common)
    return _scan_call(reverse=True, final_args=(o_fwd, 3, ret_gn), **common)


def _gla(p_gla, w_decay, b_decay, gla_gn, seq_block):
    dk_tot = N_HEADS * GLA_DK
    outs = []
    o_fwd = None
    for direction in range(2):
        wd = jnp.zeros((LANES, dk_tot), F32).at[direction * GLA_RANK:(direction + 1) * GLA_RANK].set(
            w_decay[direction])
        extra = [p_gla, wd, b_decay[direction].reshape(1, dk_tot)]
        lr_spec = lambda blk, bb, rows: pl.BlockSpec((bb, rows, LANES), lambda b, t: (b, blk(t), 6))
        specs = [lr_spec, _const_spec((LANES, dk_tot)), _const_spec((1, dk_tot))]
        final_args = None if direction == 0 else (o_fwd, 2, gla_gn)
        out = _scan_call(src=p_gla, cols=(0, 1, 1), extra=extra, extra_specs=specs, seq_block=seq_block,
                         chunk=GLA_CHUNK, dk_tot=dk_tot, reverse=bool(direction), is_ret=False,
                         final_args=final_args)
        o_fwd = out
        outs.append(out)
    return outs[1]


def _merge_kernel(x_ref, mod_ref, npre_ref, npost_ref, wg_ref, wb_ref, wo_ref,
                  fn_ref, ret_ref, hy_ref, gla_ref, o_ref):
    x = x_ref[0]
    h = _mod_norm(x, npre_ref[...], mod_ref[0, 1:2, :], mod_ref[0, 0:1, :]).astype(BF16)
    merged = None
    for n, br in enumerate((fn_ref, ret_ref, hy_ref, gla_ref)):
        gate = jax.nn.sigmoid(_dot(h, wg_ref[:, n * D_MODEL:(n + 1) * D_MODEL]))
        term = gate * _dot(br[0].astype(BF16), wb_ref[n])
        merged = term if merged is None else merged + term
    y = _dot(merged.astype(BF16), wo_ref[...])
    o_ref[0] = x + mod_ref[0, 2:3, :] * (_rms(y) * npost_ref[...])


def _merge(x, mod, norm_pre, norm_post, w_gate, w_branch, w_out, branches, tm):
    bn, seq, _ = x.shape
    row = lambda w: pl.BlockSpec((1, tm, w), lambda b, t: (b, t, 0))
    return pl.pallas_call(
        _merge_kernel,
        grid=(bn, seq // tm),
        in_specs=[row(D_MODEL), pl.BlockSpec((1, 6, D_MODEL), lambda b, t: (b, 0, 0)),
                  _const_spec((1, D_MODEL)), _const_spec((1, D_MODEL)),
                  _const_spec(w_gate.shape), _const_spec(w_branch.shape), _const_spec(w_out.shape)]
                 + [row(MIX_W)] * N_BRANCH,
        out_specs=row(D_MODEL),
        out_shape=jax.ShapeDtypeStruct(x.shape, F32),
        compiler_params=_cparams(2),
        name="merge",
    )(x, mod, norm_pre.reshape(1, D_MODEL), norm_post.reshape(1, D_MODEL), w_gate, w_branch, w_out,
      *branches)


FFN_COLS = 256


def _ffn_kernel(x_ref, xp_ref, xn_ref, mod_ref, npre_ref, npost_ref, wu_ref, cw_ref, cb_ref, wd_ref, o_ref,
                a_scr, act_scr):
    t = pl.program_id(1)
    x = x_ref[0]
    tm = x.shape[0]
    ext = tm + 2 * SUBLANES
    xe = jnp.concatenate([xp_ref[0], x, xn_ref[0]], axis=0)
    h = _mod_norm(xe, npre_ref[...], mod_ref[0, 4:5, :], mod_ref[0, 3:4, :])
    rows = lax.broadcasted_iota(jnp.int32, (ext, D_MODEL), 0)
    valid = jnp.logical_and(jnp.logical_or(rows >= SUBLANES, t > 0),
                            jnp.logical_or(rows < tm + SUBLANES, t < pl.num_programs(1) - 1))
    h = jnp.where(valid, h, 0.0).astype(BF16)

    def conv_cols(lo, slot):
        a_scr[slot] = _dot(h, wu_ref[:, lo:lo + FFN_COLS])
        below = a_scr[slot, SUBLANES - 1:SUBLANES - 1 + tm, :]
        mid = a_scr[slot, SUBLANES:SUBLANES + tm, :]
        above = a_scr[slot, SUBLANES + 1:SUBLANES + 1 + tm, :]
        return (below * cw_ref[0:1, lo:lo + FFN_COLS] + mid * cw_ref[1:2, lo:lo + FFN_COLS]
                + above * cw_ref[2:3, lo:lo + FFN_COLS] + cb_ref[:, lo:lo + FFN_COLS])

    for j in range(D_FF // FFN_COLS):
        gate = conv_cols(j * FFN_COLS, 0)
        val = conv_cols(D_FF + j * FFN_COLS, 1)
        act_scr[:, j * FFN_COLS:(j + 1) * FFN_COLS] = (jax.nn.gelu(gate, approximate=True) * val).astype(BF16)
    y = _dot(act_scr[...], wd_ref[...])
    o_ref[0] = x + mod_ref[0, 5:6, :] * (_rms(y) * npost_ref[...])


def _conv_ffn(x, mod, norm_pre, norm_post, ffn_up, conv_w, conv_b, ffn_down, tm):
    bn, seq, _ = x.shape
    prev_spec, next_spec = _halo_specs(tm, seq, D_MODEL)
    row = pl.BlockSpec((1, tm, D_MODEL), lambda b, t: (b, t, 0))
    return pl.pallas_call(
        _ffn_kernel,
        grid=(bn, seq // tm),
        in_specs=[row, prev_spec, next_spec, pl.BlockSpec((1, 6, D_MODEL), lambda b, t: (b, 0, 0)),
                  _const_spec((1, D_MODEL)), _const_spec((1, D_MODEL)),
                  _const_spec(ffn_up.shape), _const_spec(conv_w.shape), _const_spec((1, 2 * D_FF)),
                  _const_spec(ffn_down.shape)],
        out_specs=row,
        out_shape=jax.ShapeDtypeStruct(x.shape, F32),
        scratch_shapes=[pltpu.VMEM((2, tm + 2 * SUBLANES, FFN_COLS), F32),
                        pltpu.VMEM((tm, D_FF), BF16)],
        compiler_params=_cparams(2),
        name="conv_ffn",
    )(x, x, x, mod, norm_pre.reshape(1, D_MODEL), norm_post.reshape(1, D_MODEL), ffn_up, conv_w,
      conv_b.reshape(1, 2 * D_FF), ffn_down)


def _row_tile(seq):
    return min(512, seq)


def _encoder_layer(x, mod, lw, filters):
    seq = x.shape[1]
    tm = _row_tile(seq)
    p_fn, p_ret, p_hy, p_gla = _inproj(x, mod, lw["norm_pre_mix"], lw["w_mix"], tm)

    o_fn = _fnet(p_fn)
    o_ret = _retention(p_ret, lw["ret_gn"], tm)

    spectrum, gsum = filters
    v, x1, x2 = _hyena_conv3(p_hy, lw["hy_conv_w"], lw["hy_conv_b"], tm)
    tn = min(4096, FFT_N2 * MIX_W)
    z = _long_conv_gate(v, x1, spectrum, 0, gsum[0], lw["hy_skip"][0], tn)
    o_hy = _long_conv_gate(z, x2, spectrum, 1, gsum[1], lw["hy_skip"][1], tn)

    o_gla = _gla(p_gla, lw["gla_w_decay"], lw["gla_b_decay"], lw["gla_gn"], tm)

    x = _merge(x, mod, lw["norm_pre_mix"], lw["norm_post_mix"], lw["w_gate"], lw["w_branch"], lw["w_out"],
               (o_fn, o_ret, o_hy, o_gla), tm)
    return _conv_ffn(x, mod, lw["norm_pre_ffn"], lw["norm_post_ffn"], lw["ffn_up"], lw["ffn_conv_w"],
                     lw["ffn_conv_b"], lw["ffn_down"], tm)


def kernel(x_prompt, x_sample, c_prompt, c_sample, ada_w, ada_b, norm_pre_mix, norm_post_mix, norm_pre_ffn, norm_post_ffn, w_in, hy_conv_w, hy_conv_b, flt_w1, flt_b1, flt_freq, flt_w2, flt_b2, flt_w3, flt_b3, hy_skip, gla_w_decay, gla_b_decay, ret_gn, gla_gn, w_branch, w_out, ffn_up, ffn_conv_w, ffn_conv_b, ffn_down):
    groups = [x_prompt, x_sample]
    n_rows = [c_prompt.shape[0], c_sample.shape[0]]
    c_rows = jnp.concatenate([c_prompt, c_sample], axis=0)
    pad = (-c_rows.shape[0]) % SUBLANES
    c_rows = jnp.pad(c_rows, ((0, pad), (0, 0)))
    for i in range(DEPTH):
        lw = {
            "norm_pre_mix": norm_pre_mix[i], "norm_post_mix": norm_post_mix[i],
            "norm_pre_ffn": norm_pre_ffn[i], "norm_post_ffn": norm_post_ffn[i],
            "w_mix": jnp.pad(w_in[i][:, :N_MIX_IN], ((0, 0), (0, 2048 + GLA_IN_W - N_MIX_IN))).astype(BF16),
            "w_gate": w_in[i][:, N_MIX_IN:].astype(BF16),
            "hy_conv_w": hy_conv_w[i], "hy_conv_b": hy_conv_b[i], "hy_skip": hy_skip[i],
            "gla_w_decay": gla_w_decay[i], "gla_b_decay": gla_b_decay[i],
            "ret_gn": ret_gn[i], "gla_gn": gla_gn[i],
            "w_branch": w_branch[i].astype(BF16), "w_out": w_out[i].astype(BF16),
            "ffn_up": ffn_up[i].astype(BF16), "ffn_conv_w": ffn_conv_w[i], "ffn_conv_b": ffn_conv_b[i],
            "ffn_down": ffn_down[i].astype(BF16),
        }
        mod_all = _modulation(c_rows, ada_w[i], ada_b[i])
        filters = {}
        start = 0
        for gi, x in enumerate(groups):
            seq = x.shape[1]
            if seq not in filters:
                filters[seq] = _hyena_filters(seq, flt_w1[i], flt_b1[i], flt_freq[i], flt_w2[i], flt_b2[i],
                                              flt_w3[i], flt_b3[i])
            mod = mod_all[start:start + n_rows[gi]].reshape(n_rows[gi], 6, D_MODEL)
            start += n_rows[gi]
            groups[gi] = _encoder_layer(x, mod, lw, filters[seq])
    return tuple(groups)
```

```python
import functools
import math

import numpy as np
import jax
import jax.numpy as jnp
from jax import lax
from jax.experimental import pallas as pl
from jax.experimental.pallas import tpu as pltpu

F32 = jnp.float32
BF16 = jnp.bfloat16
HIGHEST = lax.Precision.HIGHEST

D_MODEL = 1024
DEPTH = 2
N_BRANCH = 4
MIX_W = 256
N_HEADS = 4
HEAD_V = MIX_W // N_HEADS
RET_DK = 64
GLA_DK = 32
GLA_RANK = 16
GLA_TAU = 16.0
HY_ORDER = 2
FLT_BANDS = 16
FLT_EMB = 1 + 2 * FLT_BANDS
FLT_EMB_PAD = 40
FLT_HIDDEN = 64
D_FF = 2816
GLA_CHUNK = 64
RET_CHUNK = 128
SCAN_MAX_BATCH_BLOCK = 4
SCAN_BLOCK_ROWS = 1024
SCAN_GROUP_ITEMS = 4
ROPE_BASE = 10000.0
EPS = 1e-6
N_MIX_IN = 2848
GLA_IN_W = 896

V7X_VMEM_BYTES = 64 * 1024 * 1024
VMEM_LIMIT = 52 * 1024 * 1024
SUBLANES = 8
LANES = 128
FFT_N2 = 128
INPROJ_ROWS = 1024


def _cparams(n_axes):
    return pltpu.CompilerParams(dimension_semantics=("arbitrary",) * n_axes,
                                vmem_limit_bytes=VMEM_LIMIT)


def _const_spec(shape):
    zeros = (0,) * len(shape)
    return pl.BlockSpec(shape, lambda *_: zeros, pipeline_mode=pl.Buffered(1))


def _dot(a, b):
    return jnp.dot(a, b, preferred_element_type=F32)


def _dot_nt(a, b):
    return lax.dot_general(a, b, (((1,), (1,)), ((), ())), preferred_element_type=F32)


def _dot_tn(a, b):
    return lax.dot_general(a, b, (((0,), (0,)), ((), ())), preferred_element_type=F32)


def _rms(x):
    return x * lax.rsqrt(jnp.mean(x * x, axis=-1, keepdims=True) + EPS)


def _silu(x):
    return x * jax.nn.sigmoid(x)


def _mod_kernel(c_ref, w_ref, b_ref, o_ref):
    c = c_ref[...]
    o_ref[...] = jnp.dot(_silu(c), w_ref[...], precision=HIGHEST, preferred_element_type=F32) + b_ref[...]


def _modulation(c_rows, ada_w, ada_b):
    rows = c_rows.shape[0]
    n_out = ada_w.shape[1]
    tn = D_MODEL
    return pl.pallas_call(
        _mod_kernel,
        grid=(n_out // tn,),
        in_specs=[pl.BlockSpec((rows, D_MODEL), lambda j: (0, 0)),
                  pl.BlockSpec((D_MODEL, tn), lambda j: (0, j)),
                  pl.BlockSpec((1, tn), lambda j: (0, j))],
        out_specs=pl.BlockSpec((rows, tn), lambda j: (0, j)),
        out_shape=jax.ShapeDtypeStruct((rows, n_out), F32),
        compiler_params=_cparams(1),
        name="modulation",
    )(c_rows, ada_w, ada_b.reshape(1, n_out))


def _mod_norm(x, norm_w, scale, shift):
    return _rms(x) * norm_w * (1.0 + scale) + shift


_MIX_SPLITS = ((0, 256), (256, 1280), (1280, 2048), (2048, 2048 + GLA_IN_W))


def _inproj_kernel(x_ref, mod_ref, nw_ref, w_ref, ofn_ref, oret_ref, ohy_ref, ogla_ref):
    h = _mod_norm(x_ref[0], nw_ref[...], mod_ref[0, 1:2, :], mod_ref[0, 0:1, :]).astype(BF16)
    for o_ref, (lo, hi) in zip((ofn_ref, oret_ref, ohy_ref, ogla_ref), _MIX_SPLITS):
        o_ref[0] = _dot(h, w_ref[:, lo:hi])


def _inproj(x, mod, norm_w, w_mix, tm):
    bn, seq, _ = x.shape
    widths = [hi - lo for lo, hi in _MIX_SPLITS]
    return pl.pallas_call(
        _inproj_kernel,
        grid=(bn, seq // tm),
        in_specs=[pl.BlockSpec((1, tm, D_MODEL), lambda b, t: (b, t, 0)),
                  pl.BlockSpec((1, 6, D_MODEL), lambda b, t: (b, 0, 0)),
                  _const_spec((1, D_MODEL)),
                  _const_spec(w_mix.shape)],
        out_specs=[pl.BlockSpec((1, tm, w), lambda b, t: (b, t, 0)) for w in widths],
        out_shape=[jax.ShapeDtypeStruct((bn, seq, w), F32) for w in widths],
        compiler_params=_cparams(2),
        name="inproj",
    )(x, mod, norm_w.reshape(1, D_MODEL), w_mix)


def _halo_specs(tm, seq, width):
    per = tm // SUBLANES
    last = seq // SUBLANES - 1
    prev_spec = pl.BlockSpec((1, SUBLANES, width), lambda b, t: (b, jnp.maximum(t * per - 1, 0), 0))
    next_spec = pl.BlockSpec((1, SUBLANES, width), lambda b, t: (b, jnp.minimum((t + 1) * per, last), 0))
    return prev_spec, next_spec


def _hyconv_kernel(u_ref, up_ref, un_ref, w_ref, b_ref, v_ref, x1_ref, x2_ref):
    t = pl.program_id(1)
    u = u_ref[0]
    tm = u.shape[0]
    prev_row = jnp.where(t > 0, up_ref[0, SUBLANES - 1:SUBLANES, :], 0.0)
    next_row = jnp.where(t < pl.num_programs(1) - 1, un_ref[0, 0:1, :], 0.0)
    rows = lax.broadcasted_iota(jnp.int32, u.shape, 0)
    below = jnp.where(rows == 0, prev_row, pltpu.roll(u, 1, 0))
    above = jnp.where(rows == tm - 1, next_row, pltpu.roll(u, tm - 1, 0))
    y = below * w_ref[0:1, :] + u * w_ref[1:2, :] + above * w_ref[2:3, :] + b_ref[...]
    v_ref[0] = y[:, 0:MIX_W]
    x1_ref[0] = y[:, MIX_W:2 * MIX_W]
    x2_ref[0] = y[:, 2 * MIX_W:3 * MIX_W]


def _hyena_conv3(u, conv_w, conv_b, tm):
    bn, seq, width = u.shape
    prev_spec, next_spec = _halo_specs(tm, seq, width)
    out = jax.ShapeDtypeStruct((bn, seq, MIX_W), F32)
    return pl.pallas_call(
        _hyconv_kernel,
        grid=(bn, seq // tm),
        in_specs=[pl.BlockSpec((1, tm, width), lambda b, t: (b, t, 0)), prev_spec, next_spec,
                  _const_spec((3, width)), _const_spec((1, width))],
        out_specs=[pl.BlockSpec((1, tm, MIX_W), lambda b, t: (b, t, 0))] * 3,
        out_shape=[out, out, out],
        compiler_params=_cparams(2),
        name="hyena_conv3",
    )(u, u, u, conv_w, conv_b.reshape(1, width))


def _dft_outer(n_total, n1, k_rows):
    del n_total
    k = np.arange(n1)[:, None]
    n = np.arange(k_rows)[None, :]
    ang = 2.0 * np.pi * ((k * n) % n1) / n1
    return jnp.asarray(np.concatenate([np.cos(ang), -np.sin(ang)], axis=0), dtype=F32)


def _idft_outer(n1, out_rows):
    n = np.arange(out_rows)[:, None]
    k = np.arange(n1)[None, :]
    ang = 2.0 * np.pi * ((k * n) % n1) / n1
    return jnp.asarray(np.concatenate([np.cos(ang), -np.sin(ang)], axis=1), dtype=F32)


def _dft_inner_tables(n_total, n1):
    n2 = np.arange(FFT_N2)
    ang = 2.0 * np.pi * ((n2[:, None] * n2[None, :]) % FFT_N2) / FFT_N2
    k1 = np.arange(n1)[:, None]
    tw = 2.0 * np.pi * ((k1 * n2[None, :]) % n_total) / n_total
    as32 = lambda a: jnp.asarray(a, dtype=F32)
    return as32(np.cos(ang)), as32(np.sin(ang)), as32(np.cos(tw)), as32(np.sin(tw))


def _lmul_kernel(f_ref, x_ref, o_ref):
    o_ref[0] = _dot(f_ref[...].astype(BF16), x_ref[0].astype(BF16)).astype(o_ref.dtype)


def _fft_outer(x_view, fmat, tn):
    bn, k_rows, width = x_view.shape
    r_rows = fmat.shape[0]
    return pl.pallas_call(
        _lmul_kernel,
        grid=(bn, width // tn),
        in_specs=[_const_spec(fmat.shape),
                  pl.BlockSpec((1, k_rows, tn), lambda b, j: (b, 0, j))],
        out_specs=pl.BlockSpec((1, r_rows, tn), lambda b, j: (b, 0, j)),
        out_shape=jax.ShapeDtypeStruct((bn, r_rows, width), BF16),
        compiler_params=_cparams(2),
        name="fft_outer",
    )(fmat, x_view)


def _build_inner_mats(fc_ref, fs_ref, twc_ref, tws_ref, m_scr, mt_scr, tk1):
    fc = fc_ref[...]
    fs = fs_ref[...]
    for i in range(tk1):
        tc = twc_ref[i:i + 1, :]
        ts = tws_ref[i:i + 1, :]
        ar = fc * tc - fs * ts
        ai = -(fc * ts + fs * tc)
        m_scr[i] = jnp.concatenate(
            [jnp.concatenate([ar, -ai], axis=1), jnp.concatenate([ai, ar], axis=1)], axis=0).astype(BF16)
        if mt_scr is not None:
            art = ar.T
            ait = ai.T
            mt_scr[i] = jnp.concatenate(
                [jnp.concatenate([art, ait], axis=1), jnp.concatenate([-ait, art], axis=1)], axis=0).astype(BF16)


def _inner_forward(m_scr, y_ref, i):
    ycat = jnp.concatenate([y_ref[0, 0, i], y_ref[0, 1, i]], axis=0)
    z = _dot(m_scr[i], ycat)
    return z[:FFT_N2], z[FFT_N2:]


def _spec_mid_kernel(tk1, fc_ref, fs_ref, twc_ref, tws_ref, y_ref, o_ref, m_scr):
    _build_inner_mats(fc_ref, fs_ref, twc_ref, tws_ref, m_scr, None, tk1)
    for i in range(tk1):
        zr, zi = _inner_forward(m_scr, y_ref, i)
        o_ref[i, 0] = zr
        o_ref[i, 1] = zi


def _conv_mid_kernel(tk1, fc_ref, fs_ref, twc_ref, tws_ref, y_ref, g_ref, o_ref, m_scr, mt_scr):
    @pl.when(pl.program_id(1) == 0)
    def _():
        _build_inner_mats(fc_ref, fs_ref, twc_ref, tws_ref, m_scr, mt_scr, tk1)

    for i in range(tk1):
        zr, zi = _inner_forward(m_scr, y_ref, i)
        gr = g_ref[i, 0]
        gi = g_ref[i, 1]
        pcat = jnp.concatenate([zr * gr - zi * gi, zr * gi + zi * gr], axis=0).astype(BF16)
        q = _dot(mt_scr[i], pcat)
        o_ref[0, 0, i] = q[:FFT_N2].astype(BF16)
        o_ref[0, 1, i] = q[FFT_N2:].astype(BF16)


def _fnet_mid_kernel(tk1, scale, fc_ref, fs_ref, twc_ref, tws_ref, y_ref, cs_ref, o_ref, m_scr):
    @pl.when(pl.program_id(1) == 0)
    def _():
        _build_inner_mats(fc_ref, fs_ref, twc_ref, tws_ref, m_scr, None, tk1)

    for i in range(tk1):
        zr, zi = _inner_forward(m_scr, y_ref, i)
        zcat = jnp.concatenate([zr, zi], axis=1).astype(BF16)
        o_ref[0, :, i * MIX_W:(i + 1) * MIX_W] = _dot(zcat, cs_ref[...].astype(BF16)) * scale


def _inner_table_specs(tk1):
    return [_const_spec((FFT_N2, FFT_N2)), _const_spec((FFT_N2, FFT_N2)),
            pl.BlockSpec((tk1, FFT_N2), lambda k, b: (k, 0)),
            pl.BlockSpec((tk1, FFT_N2), lambda k, b: (k, 0))]


FFT_TK1 = 8


def _filter_spectrum(g_raw, seq):
    n_total, chans = g_raw.shape
    n1 = n_total // FFT_N2
    y = _fft_outer(g_raw.reshape(1, n1, FFT_N2 * chans), _dft_outer(n_total, n1, n1), 4096)
    y = y.reshape(1, 2, n1, FFT_N2, chans)
    tables = _dft_inner_tables(n_total, n1)
    tk1 = FFT_TK1
    return pl.pallas_call(
        functools.partial(_spec_mid_kernel, tk1),
        grid=(n1 // tk1, 1),
        in_specs=_inner_table_specs(tk1) + [
            pl.BlockSpec((1, 2, tk1, FFT_N2, chans), lambda k, b: (0, 0, k, 0, 0))],
        out_specs=pl.BlockSpec((tk1, 2, FFT_N2, chans), lambda k, b: (k, 0, 0, 0)),
        out_shape=jax.ShapeDtypeStruct((n1, 2, FFT_N2, chans), F32),
        scratch_shapes=[pltpu.VMEM((tk1, 2 * FFT_N2, 2 * FFT_N2), BF16)],
        compiler_params=_cparams(2),
        name="filter_spectrum",
    )(*tables, y)


def _hy_out_kernel(n_total, f_ref, q_ref, v_ref, x_ref, gs_ref, sk_ref, o_ref):
    y = _dot(f_ref[...].astype(BF16), q_ref[0])
    inv = 1.0 / ((gs_ref[...] + EPS) * n_total)
    o_ref[0] = x_ref[0] * (y * inv + sk_ref[...] * v_ref[0])


def _long_conv_gate(v, x_gate, spectrum, order, gsum, skip, tn):
    bn, seq, chans = v.shape
    n_total = 2 * seq
    n1 = n_total // FFT_N2
    half = n1 // 2
    width = FFT_N2 * chans
    v_view = v.reshape(bn, half, width)
    y = _fft_outer(v_view, _dft_outer(n_total, n1, half), tn).reshape(bn, 2, n1, FFT_N2, chans)
    tables = _dft_inner_tables(n_total, n1)
    tk1 = FFT_TK1
    q = pl.pallas_call(
        functools.partial(_conv_mid_kernel, tk1),
        grid=(n1 // tk1, bn),
        in_specs=_inner_table_specs(tk1) + [
            pl.BlockSpec((1, 2, tk1, FFT_N2, chans), lambda k, b: (b, 0, k, 0, 0)),
            pl.BlockSpec((tk1, 2, FFT_N2, chans), lambda k, b: (k, 0, 0, order))],
        out_specs=pl.BlockSpec((1, 2, tk1, FFT_N2, chans), lambda k, b: (b, 0, k, 0, 0)),
        out_shape=jax.ShapeDtypeStruct((bn, 2, n1, FFT_N2, chans), BF16),
        scratch_shapes=[pltpu.VMEM((tk1, 2 * FFT_N2, 2 * FFT_N2), BF16),
                        pltpu.VMEM((tk1, 2 * FFT_N2, 2 * FFT_N2), BF16)],
        compiler_params=_cparams(2),
        name="long_conv_mid",
    )(*tables, y, spectrum)
    q_view = q.reshape(bn, 2 * n1, width)
    reps = tn // chans
    gs_row = jnp.tile(gsum.reshape(1, chans), (1, reps))
    sk_row = jnp.tile(skip.reshape(1, chans), (1, reps))
    fmat = _idft_outer(n1, half)
    out = pl.pallas_call(
        functools.partial(_hy_out_kernel, float(n_total)),
        grid=(bn, width // tn),
        in_specs=[_const_spec(fmat.shape),
                  pl.BlockSpec((1, 2 * n1, tn), lambda b, j: (b, 0, j)),
                  pl.BlockSpec((1, half, tn), lambda b, j: (b, 0, j)),
                  pl.BlockSpec((1, half, tn), lambda b, j: (b, 0, j)),
                  _const_spec((1, tn)), _const_spec((1, tn))],
        out_specs=pl.BlockSpec((1, half, tn), lambda b, j: (b, 0, j)),
        out_shape=jax.ShapeDtypeStruct((bn, half, width), F32),
        compiler_params=_cparams(2),
        name="long_conv_out",
    )(fmat, q_view, v_view, x_gate.reshape(bn, half, width), gs_row, sk_row)
    return out.reshape(bn, seq, chans)


def _filter_kernel(seq, feat_ref, w1_ref, b1_ref, fq_ref, w2_ref, b2_ref, w3_ref, b3_ref, dl_ref,
                   g_ref, s_ref):
    i = pl.program_id(0)
    f = feat_ref[...]
    tm = f.shape[0]
    fq = fq_ref[...]
    hp = functools.partial(jnp.dot, precision=HIGHEST, preferred_element_type=F32)
    h = jnp.sin(fq * (hp(f, w1_ref[...]) + b1_ref[...]))
    h = jnp.sin(fq * (hp(h, w2_ref[...]) + b2_ref[...]))
    filt = hp(h, w3_ref[...]) + b3_ref[...]
    win = jnp.exp(-f[:, 0:1] * dl_ref[...])
    r = i * tm + lax.broadcasted_iota(jnp.int32, (tm, MIX_W), 0)
    sums = []
    for o in range(HY_ORDER):
        hf = filt[:, (2 * o) * MIX_W:(2 * o + 1) * MIX_W]
        hb = filt[:, (2 * o + 1) * MIX_W:(2 * o + 2) * MIX_W]
        g = jnp.where(r < seq, hf, jnp.where(r > seq, hb, 0.0)) * win
        g_ref[:, o * MIX_W:(o + 1) * MIX_W] = g
        sums.append(jnp.sum(jnp.abs(g), axis=0, keepdims=True))
    total = jnp.concatenate(sums, axis=1)

    @pl.when(i == 0)
    def _():
        s_ref[...] = total

    @pl.when(i > 0)
    def _():
        s_ref[...] = s_ref[...] + total


def _hyena_filter_features(seq):
    t = jnp.linspace(0.0, 1.0, seq, dtype=F32)[:, None]
    w = 2.0 * math.pi * jnp.arange(seq, dtype=F32)[:, None] / seq
    f = jnp.linspace(1e-4, FLT_BANDS - 1, FLT_BANDS, dtype=F32)[None, :]
    feat = jnp.concatenate([t, jnp.cos(f * w), -jnp.sin(f * w)], axis=-1)
    feat = jnp.concatenate([feat, jnp.zeros((1, FLT_EMB), F32), feat[1:][::-1]], axis=0)
    return jnp.pad(feat, ((0, 0), (0, FLT_EMB_PAD - FLT_EMB)))


def _hyena_filters(seq, flt_w1, flt_b1, flt_freq, flt_w2, flt_b2, flt_w3, flt_b3):
    n_total = 2 * seq
    tm = 1024
    feat = _hyena_filter_features(seq)
    deltas = jnp.abs(jnp.linspace(math.log(1e-2) / 0.3, math.log(1e-2) / 1.5, MIX_W, dtype=F32))
    w1 = jnp.pad(flt_w1, ((0, FLT_EMB_PAD - FLT_EMB), (0, 0)))
    n_out = HY_ORDER * 2 * MIX_W
    g_raw, gsum = pl.pallas_call(
        functools.partial(_filter_kernel, seq),
        grid=(n_total // tm,),
        in_specs=[pl.BlockSpec((tm, FLT_EMB_PAD), lambda i: (i, 0)),
                  _const_spec((FLT_EMB_PAD, FLT_HIDDEN)), _const_spec((1, FLT_HIDDEN)),
                  _const_spec((1, FLT_HIDDEN)),
                  _const_spec((FLT_HIDDEN, FLT_HIDDEN)), _const_spec((1, FLT_HIDDEN)),
                  _const_spec((FLT_HIDDEN, n_out)), _const_spec((1, n_out)),
                  _const_spec((1, MIX_W))],
        out_specs=[pl.BlockSpec((tm, HY_ORDER * MIX_W), lambda i: (i, 0)),
                   pl.BlockSpec((1, HY_ORDER * MIX_W), lambda i: (0, 0))],
        out_shape=[jax.ShapeDtypeStruct((n_total, HY_ORDER * MIX_W), F32),
                   jax.ShapeDtypeStruct((1, HY_ORDER * MIX_W), F32)],
        compiler_params=_cparams(1),
        name="hyena_filter",
    )(feat, w1, flt_b1.reshape(1, -1), flt_freq.reshape(1, -1), flt_w2, flt_b2.reshape(1, -1),
      flt_w3, flt_b3.reshape(1, -1), deltas.reshape(1, MIX_W))
    return _filter_spectrum(g_raw, seq), gsum.reshape(HY_ORDER, MIX_W)


def _fnet_channel_mats():
    c = np.arange(MIX_W)
    same = (c[:, None] // HEAD_V) == (c[None, :] // HEAD_V)
    ang = 2.0 * np.pi * (((c[:, None] % HEAD_V) * (c[None, :] % HEAD_V)) % HEAD_V) / HEAD_V
    cs = np.concatenate([np.where(same, np.cos(ang), 0.0), np.where(same, np.sin(ang), 0.0)], axis=0)
    return jnp.asarray(cs, dtype=F32)


def _fnet(u):
    bn, seq, chans = u.shape
    n1 = seq // FFT_N2
    width = FFT_N2 * chans
    y = _fft_outer(u.reshape(bn, n1, width), _dft_outer(seq, n1, n1), 4096)
    y = y.reshape(bn, 2, n1, FFT_N2, chans)
    tables = _dft_inner_tables(seq, n1)
    tk1 = FFT_TK1
    scale = 1.0 / math.sqrt(seq * HEAD_V)
    out = pl.pallas_call(
        functools.partial(_fnet_mid_kernel, tk1, scale),
        grid=(n1 // tk1, bn),
        in_specs=_inner_table_specs(tk1) + [
            pl.BlockSpec((1, 2, tk1, FFT_N2, chans), lambda k, b: (b, 0, k, 0, 0)),
            _const_spec((2 * chans, chans))],
        out_specs=pl.BlockSpec((1, FFT_N2, tk1 * chans), lambda k, b: (b, 0, k)),
        out_shape=jax.ShapeDtypeStruct((bn, FFT_N2, n1 * chans), F32),
        scratch_shapes=[pltpu.VMEM((tk1, 2 * FFT_N2, 2 * FFT_N2), BF16)],
        compiler_params=_cparams(2),
        name="fnet_mid",
    )(*tables, y, _fnet_channel_mats())
    return out.reshape(bn, seq, chans)


def _split3(x):
    a = x.astype(BF16)
    r = x - a.astype(F32)
    b = r.astype(BF16)
    c = (r - b.astype(F32)).astype(BF16)
    return a, b, c


def _group_mean_mat():
    c = np.arange(MIX_W)
    same = (c[:, None] // HEAD_V) == (c[None, :] // HEAD_V)
    return jnp.asarray(np.where(same, 1.0 / HEAD_V, 0.0), dtype=BF16)


def _group_mean(x, avg):
    hi = x.astype(BF16)
    lo = (x - hi.astype(F32)).astype(BF16)
    return _dot(hi, avg) + _dot(lo, avg)


def _scan_kernel(cfg, *refs):
    chunk, dk_tot, reverse, is_ret, final = cfg
    refs = list(refs)
    q_ref, k_ref, v_ref = refs[:3]
    pos = 3
    if is_ret:
        cos_ref, sin_ref, lg_ref = refs[pos:pos + 3]
        pos += 3
    else:
        lr_ref, wd_ref, bd_ref = refs[pos:pos + 3]
        pos += 3
    if final:
        ofwd_ref, gate_ref, gain_ref, avg_ref = refs[pos:pos + 4]
        pos += 4
    o_ref, st_ref = refs[pos], refs[pos + 1]

    @pl.when(pl.program_id(1) == 0)
    def _():
        st_ref[...] = jnp.zeros_like(st_ref)

    dk = dk_tot // N_HEADS
    tb = q_ref.shape[1]
    n_chunks = tb // chunk
    wide = N_HEADS * chunk

    def iota(shape, axis):
        return lax.broadcasted_iota(jnp.int32, shape, axis)

    head_k = (iota((wide, dk_tot), 0) // chunk) == (iota((wide, dk_tot), 1) // dk)
    head_v = (iota((wide, MIX_W), 0) // chunk) == (iota((wide, MIX_W), 1) // HEAD_V)
    head_s = (iota((MIX_W, dk_tot), 0) // HEAD_V) == (iota((MIX_W, dk_tot), 1) // dk)
    ri = iota((chunk, wide), 0)
    ci = iota((chunk, wide), 1) % chunk
    keep = (ci > ri) if reverse else (ci <= ri)
    if not is_ret:
        ti = iota((chunk, chunk), 0)
        tj = iota((chunk, chunk), 1)
        tri = jnp.where((tj >= ti) if reverse else (tj <= ti), 1.0, 0.0).astype(BF16)
    if is_ret:
        lane = iota((chunk, MIX_W), 1)
        first_half = (lane % RET_DK) < (RET_DK // 2)
        steps = iota((chunk, dk_tot), 0)
        steps = ((chunk - steps) if reverse else (steps + 1)).astype(F32)

    if is_ret:
        lg = lg_ref[...]
        b_ret = steps * lg
        edge_ret = float(chunk) * lg
        b_mid_ret = float(chunk // 2 + 1) * lg

    bb = q_ref.shape[0]
    group_chunks = max(1, SCAN_GROUP_ITEMS // bb)

    def rotary(x, cos, sin):
        swapped = jnp.where(first_half, pltpu.roll(x, MIX_W - RET_DK // 2, 1),
                            pltpu.roll(x, RET_DK // 2, 1))
        return x * cos + swapped * sin

    def run_group(chunks, states):
        items = [(bi, slice(c * chunk, (c + 1) * chunk)) for c in chunks for bi in range(bb)]
        n = len(items)
        qs_in = [q_ref[bi, rows, :] for bi, rows in items]
        ks_in = [k_ref[bi, rows, :] for bi, rows in items]
        vs = [v_ref[bi, rows, :] for bi, rows in items]
        if is_ret:
            qs_in = [rotary(q, cos_ref[rows, :], sin_ref[rows, :]) for q, (_, rows) in zip(qs_in, items)]
            ks_in = [rotary(k, cos_ref[rows, :], sin_ref[rows, :]) for k, (_, rows) in zip(ks_in, items)]
            bs, edges, mids = [b_ret] * n, [edge_ret] * n, [b_mid_ret] * n
        else:
            lr = jnp.concatenate([lr_ref[bi, rows, :] for bi, rows in items], axis=0)
            z = jnp.dot(lr, wd_ref[...], precision=HIGHEST, preferred_element_type=F32) + bd_ref[...]
            g = (jnp.minimum(z, 0.0) - jnp.log1p(jnp.exp(-jnp.abs(z)))) * (1.0 / GLA_TAU)
            parts = []
            for i in range(n):
                parts.extend(_split3(g[i * chunk:(i + 1) * chunk]))
            cum = _dot(tri, jnp.concatenate(parts, axis=1))
            bs = [cum[:, (3 * i) * dk_tot:(3 * i + 1) * dk_tot] + cum[:, (3 * i + 1) * dk_tot:(3 * i + 2) * dk_tot]
                  + cum[:, (3 * i + 2) * dk_tot:(3 * i + 3) * dk_tot] for i in range(n)]
            edges = [b[0:1, :] if reverse else b[chunk - 1:chunk, :] for b in bs]
            mids = [b[chunk // 2:chunk // 2 + 1, :] for b in bs]
        qs_in = [q * (dk ** -0.5) for q in qs_in]
        q_intra = [(q * jnp.exp(b - m)).astype(BF16) for q, b, m in zip(qs_in, bs, mids)]
        k_intra = [k * jnp.exp(m - b) for k, b, m in zip(ks_in, bs, mids)]
        q_inter = [(q * jnp.exp(b)).astype(BF16) for q, b in zip(qs_in, bs)]
        k_state = [(k * jnp.exp(e - b)).astype(BF16) for k, b, e in zip(ks_in, bs, edges)]
        k_bd = [jnp.where(head_k, jnp.concatenate([k] * N_HEADS, axis=0), 0.0).astype(BF16) for k in k_intra]
        v_bd = [jnp.where(head_v, jnp.concatenate([v] * N_HEADS, axis=0), 0.0).astype(BF16) for v in vs]
        scores = [_dot_nt(q, k) for q, k in zip(q_intra, k_bd)]
        scores = [jnp.where(keep, s, 0.0).astype(BF16) for s in scores]
        intra = [_dot(s, v) for s, v in zip(scores, v_bd)]
        updates = [jnp.where(head_s, _dot_tn(v.astype(BF16), k), 0.0) for v, k in zip(vs, k_state)]
        decays = [jnp.exp(e) for e in edges]
        outs = []
        for i, (bi, _) in enumerate(items):
            outs.append(intra[i] + _dot_nt(q_inter[i], states[bi].astype(BF16)))
            states[bi] = states[bi] * decays[i] + updates[i]
        if final:
            o = jnp.concatenate([o + ofwd_ref[bi, rows, :] for o, (bi, rows) in zip(outs, items)], axis=0)
            gate = jnp.concatenate([gate_ref[bi, rows, :] for bi, rows in items], axis=0)
            avg = avg_ref[...]
            if is_ret:
                o = o - _group_mean(o, avg)
            o = o * lax.rsqrt(_group_mean(o * o, avg) + EPS)
            o = o * gain_ref[...] * _silu(gate)
            outs = [o[i * chunk:(i + 1) * chunk] for i in range(n)]
        for o, (bi, rows) in zip(outs, items):
            o_ref[bi, rows, :] = o

    states = [st_ref[bi] for bi in range(bb)]
    order = list(range(n_chunks - 1, -1, -1) if reverse else range(n_chunks))
    for g0 in range(0, n_chunks, group_chunks):
        run_group(order[g0:g0 + group_chunks], states)
    for bi in range(bb):
        st_ref[bi] = states[bi]


def _scan_call(src, cols, extra, extra_specs, seq_block, chunk, dk_tot, reverse, is_ret, final_args):
    bn, seq, _ = src.shape
    bb = bn if bn <= SCAN_MAX_BATCH_BLOCK else SCAN_MAX_BATCH_BLOCK
    seq_block = min(seq, SCAN_BLOCK_ROWS // bb)
    nb = seq // seq_block
    blk = (lambda t: nb - 1 - t) if reverse else (lambda t: t)
    row_spec = lambda w, col: pl.BlockSpec((bb, seq_block, w), lambda b, t: (b, blk(t), col))
    in_specs = [row_spec(dk_tot, cols[0]), row_spec(dk_tot, cols[1]), row_spec(MIX_W, cols[2])]
    args = [src, src, src]
    for a, s in zip(extra, extra_specs):
        args.append(a)
        in_specs.append(s(blk, bb, seq_block) if callable(s) else s)
    final = final_args is not None
    if final:
        o_fwd, gate_col, gain = final_args
        args += [o_fwd, src, gain.reshape(1, MIX_W), _group_mean_mat()]
        in_specs += [row_spec(MIX_W, 0), row_spec(MIX_W, gate_col), _const_spec((1, MIX_W)),
                     _const_spec((MIX_W, MIX_W))]
    cfg = (chunk, dk_tot, reverse, is_ret, final)
    return pl.pallas_call(
        functools.partial(_scan_kernel, cfg),
        grid=(bn // bb, nb),
        in_specs=in_specs,
        out_specs=pl.BlockSpec((bb, seq_block, MIX_W), lambda b, t: (b, blk(t), 0)),
        out_shape=jax.ShapeDtypeStruct((bn, seq, MIX_W), F32),
        scratch_shapes=[pltpu.VMEM((bb, MIX_W, dk_tot), F32)],
        compiler_params=_cparams(2),
        name=("ret" if is_ret else "gla") + ("_bwd" if reverse else "_fwd"),
    )(*args)


def _rotary_tables(seq):
    half = RET_DK // 2
    inv = ROPE_BASE ** (-jnp.arange(half, dtype=F32) / half)
    ang = jnp.arange(seq, dtype=F32)[:, None] * inv[None, :]
    cos = jnp.cos(ang)
    sin = jnp.sin(ang)
    cos_t = jnp.tile(jnp.concatenate([cos, cos], axis=1), (1, N_HEADS))
    sin_t = jnp.tile(jnp.concatenate([-sin, sin], axis=1), (1, N_HEADS))
    return cos_t, sin_t


def _retention(p_ret, ret_gn, seq_block):
    bn, seq, _ = p_ret.shape
    cos_t, sin_t = _rotary_tables(seq)
    log_gamma = jnp.log(1.0 - 2.0 ** (-5.0 - jnp.arange(N_HEADS, dtype=F32)))
    lg = jnp.repeat(log_gamma, RET_DK).reshape(1, MIX_W)
    extra = [cos_t, sin_t, lg]
    tab = lambda blk, bb, rows: pl.BlockSpec((rows, MIX_W), lambda b, t: (blk(t), 0))
    specs = [tab, tab, _const_spec((1, MIX_W))]
    common = dict(src=p_ret, cols=(0, 1, 2), extra=extra, extra_specs=specs, seq_block=seq_block,
                  chunk=RET_CHUNK, dk_tot=N_HEADS * RET_DK, is_ret=True)
    o_fwd = _scan_call(reverse=False, final_args=None, **common)
    return _scan_call(reverse=True, final_args=(o_fwd, 3, ret_gn), **common)


def _gla(p_gla, w_decay, b_decay, gla_gn, seq_block):
    dk_tot = N_HEADS * GLA_DK
    outs = []
    o_fwd = None
    for direction in range(2):
        wd = jnp.zeros((LANES, dk_tot), F32).at[direction * GLA_RANK:(direction + 1) * GLA_RANK].set(
            w_decay[direction])
        extra = [p_gla, wd, b_decay[direction].reshape(1, dk_tot)]
        lr_spec = lambda blk, bb, rows: pl.BlockSpec((bb, rows, LANES), lambda b, t: (b, blk(t), 6))
        specs = [lr_spec, _const_spec((LANES, dk_tot)), _const_spec((1, dk_tot))]
        final_args = None if direction == 0 else (o_fwd, 2, gla_gn)
        out = _scan_call(src=p_gla, cols=(0, 1, 1), extra=extra, extra_specs=specs, seq_block=seq_block,
                         chunk=GLA_CHUNK, dk_tot=dk_tot, reverse=bool(direction), is_ret=False,
                         final_args=final_args)
        o_fwd = out
        outs.append(out)
    return outs[1]


def _merge_kernel(x_ref, mod_ref, npre_ref, npost_ref, wg_ref, wb_ref, wo_ref,
                  fn_ref, ret_ref, hy_ref, gla_ref, o_ref):
    x = x_ref[0]
    h = _mod_norm(x, npre_ref[...], mod_ref[0, 1:2, :], mod_ref[0, 0:1, :]).astype(BF16)
    merged = None
    for n, br in enumerate((fn_ref, ret_ref, hy_ref, gla_ref)):
        gate = jax.nn.sigmoid(_dot(h, wg_ref[:, n * D_MODEL:(n + 1) * D_MODEL]))
        term = gate * _dot(br[0].astype(BF16), wb_ref[n])
        merged = term if merged is None else merged + term
    y = _dot(merged.astype(BF16), wo_ref[...])
    o_ref[0] = x + mod_ref[0, 2:3, :] * (_rms(y) * npost_ref[...])


def _merge(x, mod, norm_pre, norm_post, w_gate, w_branch, w_out, branches, tm):
    bn, seq, _ = x.shape
    row = lambda w: pl.BlockSpec((1, tm, w), lambda b, t: (b, t, 0))
    return pl.pallas_call(
        _merge_kernel,
        grid=(bn, seq // tm),
        in_specs=[row(D_MODEL), pl.BlockSpec((1, 6, D_MODEL), lambda b, t: (b, 0, 0)),
                  _const_spec((1, D_MODEL)), _const_spec((1, D_MODEL)),
                  _const_spec(w_gate.shape), _const_spec(w_branch.shape), _const_spec(w_out.shape)]
                 + [row(MIX_W)] * N_BRANCH,
        out_specs=row(D_MODEL),
        out_shape=jax.ShapeDtypeStruct(x.shape, F32),
        compiler_params=_cparams(2),
        name="merge",
    )(x, mod, norm_pre.reshape(1, D_MODEL), norm_post.reshape(1, D_MODEL), w_gate, w_branch, w_out,
      *branches)


FFN_COLS = 256


def _ffn_kernel(x_ref, xp_ref, xn_ref, mod_ref, npre_ref, npost_ref, wu_ref, cw_ref, cb_ref, wd_ref, o_ref,
                a_scr, act_scr):
    t = pl.program_id(1)
    x = x_ref[0]
    tm = x.shape[0]
    ext = tm + 2 * SUBLANES
    xe = jnp.concatenate([xp_ref[0], x, xn_ref[0]], axis=0)
    h = _mod_norm(xe, npre_ref[...], mod_ref[0, 4:5, :], mod_ref[0, 3:4, :])
    rows = lax.broadcasted_iota(jnp.int32, (ext, D_MODEL), 0)
    valid = jnp.logical_and(jnp.logical_or(rows >= SUBLANES, t > 0),
                            jnp.logical_or(rows < tm + SUBLANES, t < pl.num_programs(1) - 1))
    h = jnp.where(valid, h, 0.0).astype(BF16)

    def conv_cols(lo, slot):
        a_scr[slot] = _dot(h, wu_ref[:, lo:lo + FFN_COLS])
        below = a_scr[slot, SUBLANES - 1:SUBLANES - 1 + tm, :]
        mid = a_scr[slot, SUBLANES:SUBLANES + tm, :]
        above = a_scr[slot, SUBLANES + 1:SUBLANES + 1 + tm, :]
        return (below * cw_ref[0:1, lo:lo + FFN_COLS] + mid * cw_ref[1:2, lo:lo + FFN_COLS]
                + above * cw_ref[2:3, lo:lo + FFN_COLS] + cb_ref[:, lo:lo + FFN_COLS])

    for j in range(D_FF // FFN_COLS):
        gate = conv_cols(j * FFN_COLS, 0)
        val = conv_cols(D_FF + j * FFN_COLS, 1)
        act_scr[:, j * FFN_COLS:(j + 1) * FFN_COLS] = (jax.nn.gelu(gate, approximate=True) * val).astype(BF16)
    y = _dot(act_scr[...], wd_ref[...])
    o_ref[0] = x + mod_ref[0, 5:6, :] * (_rms(y) * npost_ref[...])


def _conv_ffn(x, mod, norm_pre, norm_post, ffn_up, conv_w, conv_b, ffn_down, tm):
    bn, seq, _ = x.shape
    prev_spec, next_spec = _halo_specs(tm, seq, D_MODEL)
    row = pl.BlockSpec((1, tm, D_MODEL), lambda b, t: (b, t, 0))
    return pl.pallas_call(
        _ffn_kernel,
        grid=(bn, seq // tm),
        in_specs=[row, prev_spec, next_spec, pl.BlockSpec((1, 6, D_MODEL), lambda b, t: (b, 0, 0)),
                  _const_spec((1, D_MODEL)), _const_spec((1, D_MODEL)),
                  _const_spec(ffn_up.shape), _const_spec(conv_w.shape), _const_spec((1, 2 * D_FF)),
                  _const_spec(ffn_down.shape)],
        out_specs=row,
        out_shape=jax.ShapeDtypeStruct(x.shape, F32),
        scratch_shapes=[pltpu.VMEM((2, tm + 2 * SUBLANES, FFN_COLS), F32),
                        pltpu.VMEM((tm, D_FF), BF16)],
        compiler_params=_cparams(2),
        name="conv_ffn",
    )(x, x, x, mod, norm_pre.reshape(1, D_MODEL), norm_post.reshape(1, D_MODEL), ffn_up, conv_w,
      conv_b.reshape(1, 2 * D_FF), ffn_down)


def _row_tile(seq):
    return min(512, seq)


def _encoder_layer(x, mod, lw, filters):
    seq = x.shape[1]
    tm = _row_tile(seq)
    p_fn, p_ret, p_hy, p_gla = _inproj(x, mod, lw["norm_pre_mix"], lw["w_mix"], min(INPROJ_ROWS, seq))

    o_fn = _fnet(p_fn)
    o_ret = _retention(p_ret, lw["ret_gn"], tm)

    spectrum, gsum = filters
    v, x1, x2 = _hyena_conv3(p_hy, lw["hy_conv_w"], lw["hy_conv_b"], tm)
    tn = min(4096, FFT_N2 * MIX_W)
    z = _long_conv_gate(v, x1, spectrum, 0, gsum[0], lw["hy_skip"][0], tn)
    o_hy = _long_conv_gate(z, x2, spectrum, 1, gsum[1], lw["hy_skip"][1], tn)

    o_gla = _gla(p_gla, lw["gla_w_decay"], lw["gla_b_decay"], lw["gla_gn"], tm)

    x = _merge(x, mod, lw["norm_pre_mix"], lw["norm_post_mix"], lw["w_gate"], lw["w_branch"], lw["w_out"],
               (o_fn, o_ret, o_hy, o_gla), tm)
    return _conv_ffn(x, mod, lw["norm_pre_ffn"], lw["norm_post_ffn"], lw["ffn_up"], lw["ffn_conv_w"],
                     lw["ffn_conv_b"], lw["ffn_down"], tm)


def kernel(x_prompt, x_sample, c_prompt, c_sample, ada_w, ada_b, norm_pre_mix, norm_post_mix, norm_pre_ffn, norm_post_ffn, w_in, hy_conv_w, hy_conv_b, flt_w1, flt_b1, flt_freq, flt_w2, flt_b2, flt_w3, flt_b3, hy_skip, gla_w_decay, gla_b_decay, ret_gn, gla_gn, w_branch, w_out, ffn_up, ffn_conv_w, ffn_conv_b, ffn_down):
    groups = [x_prompt, x_sample]
    n_rows = [c_prompt.shape[0], c_sample.shape[0]]
    c_rows = jnp.concatenate([c_prompt, c_sample], axis=0)
    pad = (-c_rows.shape[0]) % SUBLANES
    c_rows = jnp.pad(c_rows, ((0, pad), (0, 0)))
    for i in range(DEPTH):
        lw = {
            "norm_pre_mix": norm_pre_mix[i], "norm_post_mix": norm_post_mix[i],
            "norm_pre_ffn": norm_pre_ffn[i], "norm_post_ffn": norm_post_ffn[i],
            "w_mix": jnp.pad(w_in[i][:, :N_MIX_IN], ((0, 0), (0, 2048 + GLA_IN_W - N_MIX_IN))).astype(BF16),
            "w_gate": w_in[i][:, N_MIX_IN:].astype(BF16),
            "hy_conv_w": hy_conv_w[i], "hy_conv_b": hy_conv_b[i], "hy_skip": hy_skip[i],
            "gla_w_decay": gla_w_decay[i], "gla_b_decay": gla_b_decay[i],
            "ret_gn": ret_gn[i], "gla_gn": gla_gn[i],
            "w_branch": w_branch[i].astype(BF16), "w_out": w_out[i].astype(BF16),
            "ffn_up": ffn_up[i].astype(BF16), "ffn_conv_w": ffn_conv_w[i], "ffn_conv_b": ffn_conv_b[i],
            "ffn_down": ffn_down[i].astype(BF16),
        }
        mod_all = _modulation(c_rows, ada_w[i], ada_b[i])
        filters = {}
        start = 0
        for gi, x in enumerate(groups):
            seq = x.shape[1]
            if seq not in filters:
                filters[seq] = _hyena_filters(seq, flt_w1[i], flt_b1[i], flt_freq[i], flt_w2[i], flt_b2[i],
                                              flt_w3[i], flt_b3[i])
            mod = mod_all[start:start + n_rows[gi]].reshape(n_rows[gi], 6, D_MODEL)
            start += n_rows[gi]
            groups[gi] = _encoder_layer(x, mod, lw, filters[seq])
    return tuple(groups)
```

```python
import functools
import math

import numpy as np
import jax
import jax.numpy as jnp
from jax import lax
from jax.experimental import pallas as pl
from jax.experimental.pallas import tpu as pltpu

F32 = jnp.float32
BF16 = jnp.bfloat16
HIGHEST = lax.Precision.HIGHEST

D_MODEL = 1024
DEPTH = 2
N_BRANCH = 4
MIX_W = 256
N_HEADS = 4
HEAD_V = MIX_W // N_HEADS
RET_DK = 64
GLA_DK = 32
GLA_RANK = 16
GLA_TAU = 16.0
HY_ORDER = 2
FLT_BANDS = 16
FLT_EMB = 1 + 2 * FLT_BANDS
FLT_EMB_PAD = 40
FLT_HIDDEN = 64
D_FF = 2816
GLA_CHUNK = 64
RET_CHUNK = 128
SCAN_MAX_BATCH_BLOCK = 4
SCAN_BLOCK_ROWS = 1024
SCAN_GROUP_ITEMS = 8
ROPE_BASE = 10000.0
EPS = 1e-6
N_MIX_IN = 2848
GLA_IN_W = 896

V7X_VMEM_BYTES = 64 * 1024 * 1024
VMEM_LIMIT = 52 * 1024 * 1024
SUBLANES = 8
LANES = 128
FFT_N2 = 128
INPROJ_ROWS = 1024


def _cparams(n_axes):
    return pltpu.CompilerParams(dimension_semantics=("arbitrary",) * n_axes,
                                vmem_limit_bytes=VMEM_LIMIT)


def _const_spec(shape):
    zeros = (0,) * len(shape)
    return pl.BlockSpec(shape, lambda *_: zeros, pipeline_mode=pl.Buffered(1))


def _dot(a, b):
    return jnp.dot(a, b, preferred_element_type=F32)


def _dot_nt(a, b):
    return lax.dot_general(a, b, (((1,), (1,)), ((), ())), preferred_element_type=F32)


def _dot_tn(a, b):
    return lax.dot_general(a, b, (((0,), (0,)), ((), ())), preferred_element_type=F32)


def _rms(x):
    return x * lax.rsqrt(jnp.mean(x * x, axis=-1, keepdims=True) + EPS)


def _silu(x):
    return x * jax.nn.sigmoid(x)


def _mod_kernel(c_ref, w_ref, b_ref, o_ref):
    c = c_ref[...]
    o_ref[...] = jnp.dot(_silu(c), w_ref[...], precision=HIGHEST, preferred_element_type=F32) + b_ref[...]


def _modulation(c_rows, ada_w, ada_b):
    rows = c_rows.shape[0]
    n_out = ada_w.shape[1]
    tn = D_MODEL
    return pl.pallas_call(
        _mod_kernel,
        grid=(n_out // tn,),
        in_specs=[pl.BlockSpec((rows, D_MODEL), lambda j: (0, 0)),
                  pl.BlockSpec((D_MODEL, tn), lambda j: (0, j)),
                  pl.BlockSpec((1, tn), lambda j: (0, j))],
        out_specs=pl.BlockSpec((rows, tn), lambda j: (0, j)),
        out_shape=jax.ShapeDtypeStruct((rows, n_out), F32),
        compiler_params=_cparams(1),
        name="modulation",
    )(c_rows, ada_w, ada_b.reshape(1, n_out))


def _mod_norm(x, norm_w, scale, shift):
    return _rms(x) * norm_w * (1.0 + scale) + shift


_MIX_SPLITS = ((0, 256), (256, 1280), (1280, 2048), (2048, 2048 + GLA_IN_W))


def _inproj_kernel(x_ref, mod_ref, nw_ref, w_ref, ofn_ref, oret_ref, ohy_ref, ogla_ref):
    h = _mod_norm(x_ref[0], nw_ref[...], mod_ref[0, 1:2, :], mod_ref[0, 0:1, :]).astype(BF16)
    for o_ref, (lo, hi) in zip((ofn_ref, oret_ref, ohy_ref, ogla_ref), _MIX_SPLITS):
        o_ref[0] = _dot(h, w_ref[:, lo:hi])


def _inproj(x, mod, norm_w, w_mix, tm):
    bn, seq, _ = x.shape
    widths = [hi - lo for lo, hi in _MIX_SPLITS]
    return pl.pallas_call(
        _inproj_kernel,
        grid=(bn, seq // tm),
        in_specs=[pl.BlockSpec((1, tm, D_MODEL), lambda b, t: (b, t, 0)),
                  pl.BlockSpec((1, 6, D_MODEL), lambda b, t: (b, 0, 0)),
                  _const_spec((1, D_MODEL)),
                  _const_spec(w_mix.shape)],
        out_specs=[pl.BlockSpec((1, tm, w), lambda b, t: (b, t, 0)) for w in widths],
        out_shape=[jax.ShapeDtypeStruct((bn, seq, w), F32) for w in widths],
        compiler_params=_cparams(2),
        name="inproj",
    )(x, mod, norm_w.reshape(1, D_MODEL), w_mix)


def _halo_specs(tm, seq, width):
    per = tm // SUBLANES
    last = seq // SUBLANES - 1
    prev_spec = pl.BlockSpec((1, SUBLANES, width), lambda b, t: (b, jnp.maximum(t * per - 1, 0), 0))
    next_spec = pl.BlockSpec((1, SUBLANES, width), lambda b, t: (b, jnp.minimum((t + 1) * per, last), 0))
    return prev_spec, next_spec


def _hyconv_kernel(u_ref, up_ref, un_ref, w_ref, b_ref, v_ref, x1_ref, x2_ref):
    t = pl.program_id(1)
    u = u_ref[0]
    tm = u.shape[0]
    prev_row = jnp.where(t > 0, up_ref[0, SUBLANES - 1:SUBLANES, :], 0.0)
    next_row = jnp.where(t < pl.num_programs(1) - 1, un_ref[0, 0:1, :], 0.0)
    rows = lax.broadcasted_iota(jnp.int32, u.shape, 0)
    below = jnp.where(rows == 0, prev_row, pltpu.roll(u, 1, 0))
    above = jnp.where(rows == tm - 1, next_row, pltpu.roll(u, tm - 1, 0))
    y = below * w_ref[0:1, :] + u * w_ref[1:2, :] + above * w_ref[2:3, :] + b_ref[...]
    v_ref[0] = y[:, 0:MIX_W]
    x1_ref[0] = y[:, MIX_W:2 * MIX_W]
    x2_ref[0] = y[:, 2 * MIX_W:3 * MIX_W]


def _hyena_conv3(u, conv_w, conv_b, tm):
    bn, seq, width = u.shape
    prev_spec, next_spec = _halo_specs(tm, seq, width)
    out = jax.ShapeDtypeStruct((bn, seq, MIX_W), F32)
    return pl.pallas_call(
        _hyconv_kernel,
        grid=(bn, seq // tm),
        in_specs=[pl.BlockSpec((1, tm, width), lambda b, t: (b, t, 0)), prev_spec, next_spec,
                  _const_spec((3, width)), _const_spec((1, width))],
        out_specs=[pl.BlockSpec((1, tm, MIX_W), lambda b, t: (b, t, 0))] * 3,
        out_shape=[out, out, out],
        compiler_params=_cparams(2),
        name="hyena_conv3",
    )(u, u, u, conv_w, conv_b.reshape(1, width))


def _dft_outer(n_total, n1, k_rows):
    del n_total
    k = np.arange(n1)[:, None]
    n = np.arange(k_rows)[None, :]
    ang = 2.0 * np.pi * ((k * n) % n1) / n1
    return jnp.asarray(np.concatenate([np.cos(ang), -np.sin(ang)], axis=0), dtype=F32)


def _idft_outer(n1, out_rows):
    n = np.arange(out_rows)[:, None]
    k = np.arange(n1)[None, :]
    ang = 2.0 * np.pi * ((k * n) % n1) / n1
    return jnp.asarray(np.concatenate([np.cos(ang), -np.sin(ang)], axis=1), dtype=F32)


def _dft_inner_tables(n_total, n1):
    n2 = np.arange(FFT_N2)
    ang = 2.0 * np.pi * ((n2[:, None] * n2[None, :]) % FFT_N2) / FFT_N2
    k1 = np.arange(n1)[:, None]
    tw = 2.0 * np.pi * ((k1 * n2[None, :]) % n_total) / n_total
    as32 = lambda a: jnp.asarray(a, dtype=F32)
    return as32(np.cos(ang)), as32(np.sin(ang)), as32(np.cos(tw)), as32(np.sin(tw))


def _lmul_kernel(f_ref, x_ref, o_ref):
    o_ref[0] = _dot(f_ref[...].astype(BF16), x_ref[0].astype(BF16)).astype(o_ref.dtype)


def _fft_outer(x_view, fmat, tn):
    bn, k_rows, width = x_view.shape
    r_rows = fmat.shape[0]
    return pl.pallas_call(
        _lmul_kernel,
        grid=(bn, width // tn),
        in_specs=[_const_spec(fmat.shape),
                  pl.BlockSpec((1, k_rows, tn), lambda b, j: (b, 0, j))],
        out_specs=pl.BlockSpec((1, r_rows, tn), lambda b, j: (b, 0, j)),
        out_shape=jax.ShapeDtypeStruct((bn, r_rows, width), BF16),
        compiler_params=_cparams(2),
        name="fft_outer",
    )(fmat, x_view)


def _build_inner_mats(fc_ref, fs_ref, twc_ref, tws_ref, m_scr, mt_scr, tk1):
    fc = fc_ref[...]
    fs = fs_ref[...]
    for i in range(tk1):
        tc = twc_ref[i:i + 1, :]
        ts = tws_ref[i:i + 1, :]
        ar = fc * tc - fs * ts
        ai = -(fc * ts + fs * tc)
        m_scr[i] = jnp.concatenate(
            [jnp.concatenate([ar, -ai], axis=1), jnp.concatenate([ai, ar], axis=1)], axis=0).astype(BF16)
        if mt_scr is not None:
            art = ar.T
            ait = ai.T
            mt_scr[i] = jnp.concatenate(
                [jnp.concatenate([art, ait], axis=1), jnp.concatenate([-ait, art], axis=1)], axis=0).astype(BF16)


def _inner_forward(m_scr, y_ref, i):
    ycat = jnp.concatenate([y_ref[0, 0, i], y_ref[0, 1, i]], axis=0)
    z = _dot(m_scr[i], ycat)
    return z[:FFT_N2], z[FFT_N2:]


def _spec_mid_kernel(tk1, fc_ref, fs_ref, twc_ref, tws_ref, y_ref, o_ref, m_scr):
    _build_inner_mats(fc_ref, fs_ref, twc_ref, tws_ref, m_scr, None, tk1)
    for i in range(tk1):
        zr, zi = _inner_forward(m_scr, y_ref, i)
        o_ref[i, 0] = zr
        o_ref[i, 1] = zi


def _conv_mid_kernel(tk1, fc_ref, fs_ref, twc_ref, tws_ref, y_ref, g_ref, o_ref, m_scr, mt_scr):
    @pl.when(pl.program_id(1) == 0)
    def _():
        _build_inner_mats(fc_ref, fs_ref, twc_ref, tws_ref, m_scr, mt_scr, tk1)

    for i in range(tk1):
        zr, zi = _inner_forward(m_scr, y_ref, i)
        gr = g_ref[i, 0]
        gi = g_ref[i, 1]
        pcat = jnp.concatenate([zr * gr - zi * gi, zr * gi + zi * gr], axis=0).astype(BF16)
        q = _dot(mt_scr[i], pcat)
        o_ref[0, 0, i] = q[:FFT_N2].astype(BF16)
        o_ref[0, 1, i] = q[FFT_N2:].astype(BF16)


def _fnet_mid_kernel(tk1, scale, fc_ref, fs_ref, twc_ref, tws_ref, y_ref, cs_ref, o_ref, m_scr):
    @pl.when(pl.program_id(1) == 0)
    def _():
        _build_inner_mats(fc_ref, fs_ref, twc_ref, tws_ref, m_scr, None, tk1)

    for i in range(tk1):
        zr, zi = _inner_forward(m_scr, y_ref, i)
        zcat = jnp.concatenate([zr, zi], axis=1).astype(BF16)
        o_ref[0, :, i * MIX_W:(i + 1) * MIX_W] = _dot(zcat, cs_ref[...].astype(BF16)) * scale


def _inner_table_specs(tk1):
    return [_const_spec((FFT_N2, FFT_N2)), _const_spec((FFT_N2, FFT_N2)),
            pl.BlockSpec((tk1, FFT_N2), lambda k, b: (k, 0)),
            pl.BlockSpec((tk1, FFT_N2), lambda k, b: (k, 0))]


FFT_TK1 = 8


def _filter_spectrum(g_raw, seq):
    n_total, chans = g_raw.shape
    n1 = n_total // FFT_N2
    y = _fft_outer(g_raw.reshape(1, n1, FFT_N2 * chans), _dft_outer(n_total, n1, n1), 4096)
    y = y.reshape(1, 2, n1, FFT_N2, chans)
    tables = _dft_inner_tables(n_total, n1)
    tk1 = FFT_TK1
    return pl.pallas_call(
        functools.partial(_spec_mid_kernel, tk1),
        grid=(n1 // tk1, 1),
        in_specs=_inner_table_specs(tk1) + [
            pl.BlockSpec((1, 2, tk1, FFT_N2, chans), lambda k, b: (0, 0, k, 0, 0))],
        out_specs=pl.BlockSpec((tk1, 2, FFT_N2, chans), lambda k, b: (k, 0, 0, 0)),
        out_shape=jax.ShapeDtypeStruct((n1, 2, FFT_N2, chans), F32),
        scratch_shapes=[pltpu.VMEM((tk1, 2 * FFT_N2, 2 * FFT_N2), BF16)],
        compiler_params=_cparams(2),
        name="filter_spectrum",
    )(*tables, y)


def _hy_out_kernel(n_total, f_ref, q_ref, v_ref, x_ref, gs_ref, sk_ref, o_ref):
    y = _dot(f_ref[...].astype(BF16), q_ref[0])
    inv = 1.0 / ((gs_ref[...] + EPS) * n_total)
    o_ref[0] = x_ref[0] * (y * inv + sk_ref[...] * v_ref[0])


def _long_conv_gate(v, x_gate, spectrum, order, gsum, skip, tn):
    bn, seq, chans = v.shape
    n_total = 2 * seq
    n1 = n_total // FFT_N2
    half = n1 // 2
    width = FFT_N2 * chans
    v_view = v.reshape(bn, half, width)
    y = _fft_outer(v_view, _dft_outer(n_total, n1, half), tn).reshape(bn, 2, n1, FFT_N2, chans)
    tables = _dft_inner_tables(n_total, n1)
    tk1 = FFT_TK1
    q = pl.pallas_call(
        functools.partial(_conv_mid_kernel, tk1),
        grid=(n1 // tk1, bn),
        in_specs=_inner_table_specs(tk1) + [
            pl.BlockSpec((1, 2, tk1, FFT_N2, chans), lambda k, b: (b, 0, k, 0, 0)),
            pl.BlockSpec((tk1, 2, FFT_N2, chans), lambda k, b: (k, 0, 0, order))],
        out_specs=pl.BlockSpec((1, 2, tk1, FFT_N2, chans), lambda k, b: (b, 0, k, 0, 0)),
        out_shape=jax.ShapeDtypeStruct((bn, 2, n1, FFT_N2, chans), BF16),
        scratch_shapes=[pltpu.VMEM((tk1, 2 * FFT_N2, 2 * FFT_N2), BF16),
                        pltpu.VMEM((tk1, 2 * FFT_N2, 2 * FFT_N2), BF16)],
        compiler_params=_cparams(2),
        name="long_conv_mid",
    )(*tables, y, spectrum)
    q_view = q.reshape(bn, 2 * n1, width)
    reps = tn // chans
    gs_row = jnp.tile(gsum.reshape(1, chans), (1, reps))
    sk_row = jnp.tile(skip.reshape(1, chans), (1, reps))
    fmat = _idft_outer(n1, half)
    out = pl.pallas_call(
        functools.partial(_hy_out_kernel, float(n_total)),
        grid=(bn, width // tn),
        in_specs=[_const_spec(fmat.shape),
                  pl.BlockSpec((1, 2 * n1, tn), lambda b, j: (b, 0, j)),
                  pl.BlockSpec((1, half, tn), lambda b, j: (b, 0, j)),
                  pl.BlockSpec((1, half, tn), lambda b, j: (b, 0, j)),
                  _const_spec((1, tn)), _const_spec((1, tn))],
        out_specs=pl.BlockSpec((1, half, tn), lambda b, j: (b, 0, j)),
        out_shape=jax.ShapeDtypeStruct((bn, half, width), F32),
        compiler_params=_cparams(2),
        name="long_conv_out",
    )(fmat, q_view, v_view, x_gate.reshape(bn, half, width), gs_row, sk_row)
    return out.reshape(bn, seq, chans)


def _filter_kernel(feat_ref, w1_ref, b1_ref, fq_ref, w2_ref, b2_ref, w3_ref, b3_ref, dl_ref,
                   gf_ref, gb_ref, s_ref):
    i = pl.program_id(0)
    f = feat_ref[...]
    tm = f.shape[0]
    fq = fq_ref[...]
    hp = functools.partial(jnp.dot, precision=HIGHEST, preferred_element_type=F32)
    h = jnp.sin(fq * (hp(f, w1_ref[...]) + b1_ref[...]))
    h = jnp.sin(fq * (hp(h, w2_ref[...]) + b2_ref[...]))
    filt = hp(h, w3_ref[...]) + b3_ref[...]
    win = jnp.exp(-f[:, 0:1] * dl_ref[...])
    lag = i * tm + lax.broadcasted_iota(jnp.int32, (tm, MIX_W), 0)
    sums = []
    for o in range(HY_ORDER):
        hf = filt[:, (2 * o) * MIX_W:(2 * o + 1) * MIX_W] * win
        hb = jnp.where(lag > 0, filt[:, (2 * o + 1) * MIX_W:(2 * o + 2) * MIX_W] * win, 0.0)
        gf_ref[:, o * MIX_W:(o + 1) * MIX_W] = hf
        gb_ref[:, o * MIX_W:(o + 1) * MIX_W] = hb
        sums.append(jnp.sum(jnp.abs(hf) + jnp.abs(hb), axis=0, keepdims=True))
    total = jnp.concatenate(sums, axis=1)

    @pl.when(i == 0)
    def _():
        s_ref[...] = total

    @pl.when(i > 0)
    def _():
        s_ref[...] = s_ref[...] + total


def _hyena_filter_features(seq):
    t = jnp.linspace(0.0, 1.0, seq, dtype=F32)[:, None]
    w = 2.0 * math.pi * jnp.arange(seq, dtype=F32)[:, None] / seq
    f = jnp.linspace(1e-4, FLT_BANDS - 1, FLT_BANDS, dtype=F32)[None, :]
    feat = jnp.concatenate([t, jnp.cos(f * w), -jnp.sin(f * w)], axis=-1)
    return jnp.pad(feat, ((0, 0), (0, FLT_EMB_PAD - FLT_EMB)))


def _hyena_filters(seq, flt_w1, flt_b1, flt_freq, flt_w2, flt_b2, flt_w3, flt_b3):
    tm = 1024
    feat = _hyena_filter_features(seq)
    deltas = jnp.abs(jnp.linspace(math.log(1e-2) / 0.3, math.log(1e-2) / 1.5, MIX_W, dtype=F32))
    w1 = jnp.pad(flt_w1, ((0, FLT_EMB_PAD - FLT_EMB), (0, 0)))
    n_out = HY_ORDER * 2 * MIX_W
    taps = jax.ShapeDtypeStruct((seq, HY_ORDER * MIX_W), F32)
    g_fwd, g_bwd, gsum = pl.pallas_call(
        _filter_kernel,
        grid=(seq // tm,),
        in_specs=[pl.BlockSpec((tm, FLT_EMB_PAD), lambda i: (i, 0)),
                  _const_spec((FLT_EMB_PAD, FLT_HIDDEN)), _const_spec((1, FLT_HIDDEN)),
                  _const_spec((1, FLT_HIDDEN)),
                  _const_spec((FLT_HIDDEN, FLT_HIDDEN)), _const_spec((1, FLT_HIDDEN)),
                  _const_spec((FLT_HIDDEN, n_out)), _const_spec((1, n_out)),
                  _const_spec((1, MIX_W))],
        out_specs=[pl.BlockSpec((tm, HY_ORDER * MIX_W), lambda i: (i, 0)),
                   pl.BlockSpec((tm, HY_ORDER * MIX_W), lambda i: (i, 0)),
                   pl.BlockSpec((1, HY_ORDER * MIX_W), lambda i: (0, 0))],
        out_shape=[taps, taps, jax.ShapeDtypeStruct((1, HY_ORDER * MIX_W), F32)],
        compiler_params=_cparams(1),
        name="hyena_filter",
    )(feat, w1, flt_b1.reshape(1, -1), flt_freq.reshape(1, -1), flt_w2, flt_b2.reshape(1, -1),
      flt_w3, flt_b3.reshape(1, -1), deltas.reshape(1, MIX_W))
    g_raw = jnp.concatenate([g_fwd, jnp.roll(g_bwd[::-1], 1, axis=0)], axis=0)
    return _filter_spectrum(g_raw, seq), gsum.reshape(HY_ORDER, MIX_W)


def _fnet_channel_mats():
    c = np.arange(MIX_W)
    same = (c[:, None] // HEAD_V) == (c[None, :] // HEAD_V)
    ang = 2.0 * np.pi * (((c[:, None] % HEAD_V) * (c[None, :] % HEAD_V)) % HEAD_V) / HEAD_V
    cs = np.concatenate([np.where(same, np.cos(ang), 0.0), np.where(same, np.sin(ang), 0.0)], axis=0)
    return jnp.asarray(cs, dtype=F32)


def _fnet(u):
    bn, seq, chans = u.shape
    n1 = seq // FFT_N2
    width = FFT_N2 * chans
    y = _fft_outer(u.reshape(bn, n1, width), _dft_outer(seq, n1, n1), 4096)
    y = y.reshape(bn, 2, n1, FFT_N2, chans)
    tables = _dft_inner_tables(seq, n1)
    tk1 = FFT_TK1
    scale = 1.0 / math.sqrt(seq * HEAD_V)
    out = pl.pallas_call(
        functools.partial(_fnet_mid_kernel, tk1, scale),
        grid=(n1 // tk1, bn),
        in_specs=_inner_table_specs(tk1) + [
            pl.BlockSpec((1, 2, tk1, FFT_N2, chans), lambda k, b: (b, 0, k, 0, 0)),
            _const_spec((2 * chans, chans))],
        out_specs=pl.BlockSpec((1, FFT_N2, tk1 * chans), lambda k, b: (b, 0, k)),
        out_shape=jax.ShapeDtypeStruct((bn, FFT_N2, n1 * chans), F32),
        scratch_shapes=[pltpu.VMEM((tk1, 2 * FFT_N2, 2 * FFT_N2), BF16)],
        compiler_params=_cparams(2),
        name="fnet_mid",
    )(*tables, y, _fnet_channel_mats())
    return out.reshape(bn, seq, chans)


def _split3(x):
    a = x.astype(BF16)
    r = x - a.astype(F32)
    b = r.astype(BF16)
    c = (r - b.astype(F32)).astype(BF16)
    return a, b, c


def _group_mean_mat():
    c = np.arange(MIX_W)
    same = (c[:, None] // HEAD_V) == (c[None, :] // HEAD_V)
    return jnp.asarray(np.where(same, 1.0 / HEAD_V, 0.0), dtype=BF16)


def _group_mean(x, avg):
    hi = x.astype(BF16)
    lo = (x - hi.astype(F32)).astype(BF16)
    return _dot(hi, avg) + _dot(lo, avg)


def _scan_kernel(cfg, *refs):
    chunk, dk_tot, reverse, is_ret, final = cfg
    refs = list(refs)
    q_ref, k_ref, v_ref = refs[:3]
    pos = 3
    if is_ret:
        cos_ref, sin_ref, lg_ref = refs[pos:pos + 3]
        pos += 3
    else:
        lr_ref, wd_ref, bd_ref = refs[pos:pos + 3]
        pos += 3
    if final:
        ofwd_ref, gate_ref, gain_ref, avg_ref = refs[pos:pos + 4]
        pos += 4
    o_ref, st_ref = refs[pos], refs[pos + 1]

    @pl.when(pl.program_id(1) == 0)
    def _():
        st_ref[...] = jnp.zeros_like(st_ref)

    dk = dk_tot // N_HEADS
    tb = q_ref.shape[1]
    n_chunks = tb // chunk
    wide = N_HEADS * chunk

    def iota(shape, axis):
        return lax.broadcasted_iota(jnp.int32, shape, axis)

    head_k = (iota((wide, dk_tot), 0) // chunk) == (iota((wide, dk_tot), 1) // dk)
    head_v = (iota((wide, MIX_W), 0) // chunk) == (iota((wide, MIX_W), 1) // HEAD_V)
    head_s = (iota((MIX_W, dk_tot), 0) // HEAD_V) == (iota((MIX_W, dk_tot), 1) // dk)
    ri = iota((chunk, wide), 0)
    ci = iota((chunk, wide), 1) % chunk
    keep = (ci > ri) if reverse else (ci <= ri)
    if not is_ret:
        ti = iota((chunk, chunk), 0)
        tj = iota((chunk, chunk), 1)
        tri = jnp.where((tj >= ti) if reverse else (tj <= ti), 1.0, 0.0).astype(BF16)
    if is_ret:
        lane = iota((chunk, MIX_W), 1)
        first_half = (lane % RET_DK) < (RET_DK // 2)
        steps = iota((chunk, dk_tot), 0)
        steps = ((chunk - steps) if reverse else (steps + 1)).astype(F32)

    if is_ret:
        lg = lg_ref[...]
        b_ret = steps * lg
        edge_ret = float(chunk) * lg
        b_mid_ret = float(chunk // 2 + 1) * lg

    bb = q_ref.shape[0]
    group_chunks = max(1, SCAN_GROUP_ITEMS // bb)

    def rotary(x, cos, sin):
        swapped = jnp.where(first_half, pltpu.roll(x, MIX_W - RET_DK // 2, 1),
                            pltpu.roll(x, RET_DK // 2, 1))
        return x * cos + swapped * sin

    def run_group(chunks, states):
        items = [(bi, slice(c * chunk, (c + 1) * chunk)) for c in chunks for bi in range(bb)]
        n = len(items)
        qs_in = [q_ref[bi, rows, :] for bi, rows in items]
        ks_in = [k_ref[bi, rows, :] for bi, rows in items]
        vs = [v_ref[bi, rows, :] for bi, rows in items]
        if is_ret:
            qs_in = [rotary(q, cos_ref[rows, :], sin_ref[rows, :]) for q, (_, rows) in zip(qs_in, items)]
            ks_in = [rotary(k, cos_ref[rows, :], sin_ref[rows, :]) for k, (_, rows) in zip(ks_in, items)]
            bs, edges, mids = [b_ret] * n, [edge_ret] * n, [b_mid_ret] * n
        else:
            lr = jnp.concatenate([lr_ref[bi, rows, :] for bi, rows in items], axis=0)
            z = jnp.dot(lr, wd_ref[...], precision=HIGHEST, preferred_element_type=F32) + bd_ref[...]
            g = (jnp.minimum(z, 0.0) - jnp.log1p(jnp.exp(-jnp.abs(z)))) * (1.0 / GLA_TAU)
            parts = []
            for i in range(n):
                parts.extend(_split3(g[i * chunk:(i + 1) * chunk]))
            cum = _dot(tri, jnp.concatenate(parts, axis=1))
            bs = [cum[:, (3 * i) * dk_tot:(3 * i + 1) * dk_tot] + cum[:, (3 * i + 1) * dk_tot:(3 * i + 2) * dk_tot]
                  + cum[:, (3 * i + 2) * dk_tot:(3 * i + 3) * dk_tot] for i in range(n)]
            edges = [b[0:1, :] if reverse else b[chunk - 1:chunk, :] for b in bs]
            mids = [b[chunk // 2:chunk // 2 + 1, :] for b in bs]
        qs_in = [q * (dk ** -0.5) for q in qs_in]
        q_intra = [(q * jnp.exp(b - m)).astype(BF16) for q, b, m in zip(qs_in, bs, mids)]
        k_intra = [k * jnp.exp(m - b) for k, b, m in zip(ks_in, bs, mids)]
        q_inter = [(q * jnp.exp(b)).astype(BF16) for q, b in zip(qs_in, bs)]
        k_state = [(k * jnp.exp(e - b)).astype(BF16) for k, b, e in zip(ks_in, bs, edges)]
        k_bd = [jnp.where(head_k, jnp.concatenate([k] * N_HEADS, axis=0), 0.0).astype(BF16) for k in k_intra]
        v_bd = [jnp.where(head_v, jnp.concatenate([v] * N_HEADS, axis=0), 0.0).astype(BF16) for v in vs]
        scores = [_dot_nt(q, k) for q, k in zip(q_intra, k_bd)]
        scores = [jnp.where(keep, s, 0.0).astype(BF16) for s in scores]
        intra = [_dot(s, v) for s, v in zip(scores, v_bd)]
        updates = [jnp.where(head_s, _dot_tn(v.astype(BF16), k), 0.0) for v, k in zip(vs, k_state)]
        decays = [jnp.exp(e) for e in edges]
        outs = []
        for i, (bi, _) in enumerate(items):
            outs.append(intra[i] + _dot_nt(q_inter[i], states[bi].astype(BF16)))
            states[bi] = states[bi] * decays[i] + updates[i]
        if final:
            o = jnp.concatenate([o + ofwd_ref[bi, rows, :] for o, (bi, rows) in zip(outs, items)], axis=0)
            gate = jnp.concatenate([gate_ref[bi, rows, :] for bi, rows in items], axis=0)
            avg = avg_ref[...]
            if is_ret:
                o = o - _group_mean(o, avg)
            o = o * lax.rsqrt(_group_mean(o * o, avg) + EPS)
            o = o * gain_ref[...] * _silu(gate)
            outs = [o[i * chunk:(i + 1) * chunk] for i in range(n)]
        for o, (bi, rows) in zip(outs, items):
            o_ref[bi, rows, :] = o

    states = [st_ref[bi] for bi in range(bb)]
    order = list(range(n_chunks - 1, -1, -1) if reverse else range(n_chunks))
    for g0 in range(0, n_chunks, group_chunks):
        run_group(order[g0:g0 + group_chunks], states)
    for bi in range(bb):
        st_ref[bi] = states[bi]


def _scan_call(src, cols, extra, extra_specs, seq_block, chunk, dk_tot, reverse, is_ret, final_args):
    bn, seq, _ = src.shape
    bb = bn if bn <= SCAN_MAX_BATCH_BLOCK else SCAN_MAX_BATCH_BLOCK
    seq_block = min(seq, SCAN_BLOCK_ROWS // bb)
    nb = seq // seq_block
    blk = (lambda t: nb - 1 - t) if reverse else (lambda t: t)
    row_spec = lambda w, col: pl.BlockSpec((bb, seq_block, w), lambda b, t: (b, blk(t), col))
    in_specs = [row_spec(dk_tot, cols[0]), row_spec(dk_tot, cols[1]), row_spec(MIX_W, cols[2])]
    args = [src, src, src]
    for a, s in zip(extra, extra_specs):
        args.append(a)
        in_specs.append(s(blk, bb, seq_block) if callable(s) else s)
    final = final_args is not None
    if final:
        o_fwd, gate_col, gain = final_args
        args += [o_fwd, src, gain.reshape(1, MIX_W), _group_mean_mat()]
        in_specs += [row_spec(MIX_W, 0), row_spec(MIX_W, gate_col), _const_spec((1, MIX_W)),
                     _const_spec((MIX_W, MIX_W))]
    cfg = (chunk, dk_tot, reverse, is_ret, final)
    return pl.pallas_call(
        functools.partial(_scan_kernel, cfg),
        grid=(bn // bb, nb),
        in_specs=in_specs,
        out_specs=pl.BlockSpec((bb, seq_block, MIX_W), lambda b, t: (b, blk(t), 0)),
        out_shape=jax.ShapeDtypeStruct((bn, seq, MIX_W), F32),
        scratch_shapes=[pltpu.VMEM((bb, MIX_W, dk_tot), F32)],
        compiler_params=_cparams(2),
        name=("ret" if is_ret else "gla") + ("_bwd" if reverse else "_fwd"),
    )(*args)


def _rotary_tables(seq):
    half = RET_DK // 2
    inv = ROPE_BASE ** (-jnp.arange(half, dtype=F32) / half)
    ang = jnp.arange(seq, dtype=F32)[:, None] * inv[None, :]
    cos = jnp.cos(ang)
    sin = jnp.sin(ang)
    cos_t = jnp.tile(jnp.concatenate([cos, cos], axis=1), (1, N_HEADS))
    sin_t = jnp.tile(jnp.concatenate([-sin, sin], axis=1), (1, N_HEADS))
    return cos_t, sin_t


def _retention(p_ret, ret_gn, seq_block):
    bn, seq, _ = p_ret.shape
    cos_t, sin_t = _rotary_tables(seq)
    log_gamma = jnp.log(1.0 - 2.0 ** (-5.0 - jnp.arange(N_HEADS, dtype=F32)))
    lg = jnp.repeat(log_gamma, RET_DK).reshape(1, MIX_W)
    extra = [cos_t, sin_t, lg]
    tab = lambda blk, bb, rows: pl.BlockSpec((rows, MIX_W), lambda b, t: (blk(t), 0))
    specs = [tab, tab, _const_spec((1, MIX_W))]
    common = dict(src=p_ret, cols=(0, 1, 2), extra=extra, extra_specs=specs, seq_block=seq_block,
                  chunk=RET_CHUNK, dk_tot=N_HEADS * RET_DK, is_ret=True)
    o_fwd = _scan_call(reverse=False, final_args=None, **common)
    return _scan_call(reverse=True, final_args=(o_fwd, 3, ret_gn), **common)


def _gla(p_gla, w_decay, b_decay, gla_gn, seq_block):
    dk_tot = N_HEADS * GLA_DK
    outs = []
    o_fwd = None
    for direction in range(2):
        wd = jnp.zeros((LANES, dk_tot), F32).at[direction * GLA_RANK:(direction + 1) * GLA_RANK].set(
            w_decay[direction])
        extra = [p_gla, wd, b_decay[direction].reshape(1, dk_tot)]
        lr_spec = lambda blk, bb, rows: pl.BlockSpec((bb, rows, LANES), lambda b, t: (b, blk(t), 6))
        specs = [lr_spec, _const_spec((LANES, dk_tot)), _const_spec((1, dk_tot))]
        final_args = None if direction == 0 else (o_fwd, 2, gla_gn)
        out = _scan_call(src=p_gla, cols=(0, 1, 1), extra=extra, extra_specs=specs, seq_block=seq_block,
                         chunk=GLA_CHUNK, dk_tot=dk_tot, reverse=bool(direction), is_ret=False,
                         final_args=final_args)
        o_fwd = out
        outs.append(out)
    return outs[1]


def _merge_kernel(x_ref, mod_ref, npre_ref, npost_ref, wg_ref, wb_ref, wo_ref,
                  fn_ref, ret_ref, hy_ref, gla_ref, o_ref):
    x = x_ref[0]
    h = _mod_norm(x, npre_ref[...], mod_ref[0, 1:2, :], mod_ref[0, 0:1, :]).astype(BF16)
    merged = None
    for n, br in enumerate((fn_ref, ret_ref, hy_ref, gla_ref)):
        gate = jax.nn.sigmoid(_dot(h, wg_ref[:, n * D_MODEL:(n + 1) * D_MODEL]))
        term = gate * _dot(br[0].astype(BF16), wb_ref[n])
        merged = term if merged is None else merged + term
    y = _dot(merged.astype(BF16), wo_ref[...])
    o_ref[0] = x + mod_ref[0, 2:3, :] * (_rms(y) * npost_ref[...])


def _merge(x, mod, norm_pre, norm_post, w_gate, w_branch, w_out, branches, tm):
    bn, seq, _ = x.shape
    row = lambda w: pl.BlockSpec((1, tm, w), lambda b, t: (b, t, 0))
    return pl.pallas_call(
        _merge_kernel,
        grid=(bn, seq // tm),
        in_specs=[row(D_MODEL), pl.BlockSpec((1, 6, D_MODEL), lambda b, t: (b, 0, 0)),
                  _const_spec((1, D_MODEL)), _const_spec((1, D_MODEL)),
                  _const_spec(w_gate.shape), _const_spec(w_branch.shape), _const_spec(w_out.shape)]
                 + [row(MIX_W)] * N_BRANCH,
        out_specs=row(D_MODEL),
        out_shape=jax.ShapeDtypeStruct(x.shape, F32),
        compiler_params=_cparams(2),
        name="merge",
    )(x, mod, norm_pre.reshape(1, D_MODEL), norm_post.reshape(1, D_MODEL), w_gate, w_branch, w_out,
      *branches)


FFN_COLS = 256


def _ffn_kernel(x_ref, xp_ref, xn_ref, mod_ref, npre_ref, npost_ref, wu_ref, cw_ref, cb_ref, wd_ref, o_ref,
                act_scr):
    t = pl.program_id(1)
    x = x_ref[0]
    tm = x.shape[0]
    ext = tm + 2 * SUBLANES
    xe = jnp.concatenate([xp_ref[0], x, xn_ref[0]], axis=0)
    h = _mod_norm(xe, npre_ref[...], mod_ref[0, 4:5, :], mod_ref[0, 3:4, :])
    rows = lax.broadcasted_iota(jnp.int32, (ext, D_MODEL), 0)
    valid = jnp.logical_and(jnp.logical_or(rows >= SUBLANES, t > 0),
                            jnp.logical_or(rows < tm + SUBLANES, t < pl.num_programs(1) - 1))
    h = jnp.where(valid, h, 0.0).astype(BF16)

    n_tiles = tm // SUBLANES
    sub = lax.broadcasted_iota(jnp.int32, (n_tiles, SUBLANES, FFN_COLS), 1)

    def conv_cols(lo):
        a = _dot(h, wu_ref[:, lo:lo + FFN_COLS]).reshape(n_tiles + 2, SUBLANES, FFN_COLS)
        down = pltpu.roll(a, 1, 1)
        up = pltpu.roll(a, SUBLANES - 1, 1)
        below = jnp.where(sub == 0, down[0:n_tiles], down[1:n_tiles + 1])
        above = jnp.where(sub == SUBLANES - 1, up[2:n_tiles + 2], up[1:n_tiles + 1])
        mid = a[1:n_tiles + 1]
        out = (below * cw_ref[0:1, lo:lo + FFN_COLS] + mid * cw_ref[1:2, lo:lo + FFN_COLS]
               + above * cw_ref[2:3, lo:lo + FFN_COLS] + cb_ref[:, lo:lo + FFN_COLS])
        return out.reshape(tm, FFN_COLS)

    for j in range(D_FF // FFN_COLS):
        gate = conv_cols(j * FFN_COLS)
        val = conv_cols(D_FF + j * FFN_COLS)
        act_scr[:, j * FFN_COLS:(j + 1) * FFN_COLS] = (jax.nn.gelu(gate, approximate=True) * val).astype(BF16)
    y = _dot(act_scr[...], wd_ref[...])
    o_ref[0] = x + mod_ref[0, 5:6, :] * (_rms(y) * npost_ref[...])


def _conv_ffn(x, mod, norm_pre, norm_post, ffn_up, conv_w, conv_b, ffn_down, tm):
    bn, seq, _ = x.shape
    prev_spec, next_spec = _halo_specs(tm, seq, D_MODEL)
    row = pl.BlockSpec((1, tm, D_MODEL), lambda b, t: (b, t, 0))
    return pl.pallas_call(
        _ffn_kernel,
        grid=(bn, seq // tm),
        in_specs=[row, prev_spec, next_spec, pl.BlockSpec((1, 6, D_MODEL), lambda b, t: (b, 0, 0)),
                  _const_spec((1, D_MODEL)), _const_spec((1, D_MODEL)),
                  _const_spec(ffn_up.shape), _const_spec(conv_w.shape), _const_spec((1, 2 * D_FF)),
                  _const_spec(ffn_down.shape)],
        out_specs=row,
        out_shape=jax.ShapeDtypeStruct(x.shape, F32),
        scratch_shapes=[pltpu.VMEM((tm, D_FF), BF16)],
        compiler_params=_cparams(2),
        name="conv_ffn",
    )(x, x, x, mod, norm_pre.reshape(1, D_MODEL), norm_post.reshape(1, D_MODEL), ffn_up, conv_w,
      conv_b.reshape(1, 2 * D_FF), ffn_down)


def _row_tile(seq):
    return min(512, seq)


def _encoder_layer(x, mod, lw, filters):
    seq = x.shape[1]
    tm = _row_tile(seq)
    p_fn, p_ret, p_hy, p_gla = _inproj(x, mod, lw["norm_pre_mix"], lw["w_mix"], min(INPROJ_ROWS, seq))

    o_fn = _fnet(p_fn)
    o_ret = _retention(p_ret, lw["ret_gn"], tm)

    spectrum, gsum = filters
    v, x1, x2 = _hyena_conv3(p_hy, lw["hy_conv_w"], lw["hy_conv_b"], tm)
    tn = min(4096, FFT_N2 * MIX_W)
    z = _long_conv_gate(v, x1, spectrum, 0, gsum[0], lw["hy_skip"][0], tn)
    o_hy = _long_conv_gate(z, x2, spectrum, 1, gsum[1], lw["hy_skip"][1], tn)

    o_gla = _gla(p_gla, lw["gla_w_decay"], lw["gla_b_decay"], lw["gla_gn"], tm)

    x = _merge(x, mod, lw["norm_pre_mix"], lw["norm_post_mix"], lw["w_gate"], lw["w_branch"], lw["w_out"],
               (o_fn, o_ret, o_hy, o_gla), tm)
    return _conv_ffn(x, mod, lw["norm_pre_ffn"], lw["norm_post_ffn"], lw["ffn_up"], lw["ffn_conv_w"],
                     lw["ffn_conv_b"], lw["ffn_down"], tm)


def kernel(x_prompt, x_sample, c_prompt, c_sample, ada_w, ada_b, norm_pre_mix, norm_post_mix, norm_pre_ffn, norm_post_ffn, w_in, hy_conv_w, hy_conv_b, flt_w1, flt_b1, flt_freq, flt_w2, flt_b2, flt_w3, flt_b3, hy_skip, gla_w_decay, gla_b_decay, ret_gn, gla_gn, w_branch, w_out, ffn_up, ffn_conv_w, ffn_conv_b, ffn_down):
    groups = [x_prompt, x_sample]
    n_rows = [c_prompt.shape[0], c_sample.shape[0]]
    c_rows = jnp.concatenate([c_prompt, c_sample], axis=0)
    pad = (-c_rows.shape[0]) % SUBLANES
    c_rows = jnp.pad(c_rows, ((0, pad), (0, 0)))
    for i in range(DEPTH):
        lw = {
            "norm_pre_mix": norm_pre_mix[i], "norm_post_mix": norm_post_mix[i],
            "norm_pre_ffn": norm_pre_ffn[i], "norm_post_ffn": norm_post_ffn[i],
            "w_mix": jnp.pad(w_in[i][:, :N_MIX_IN], ((0, 0), (0, 2048 + GLA_IN_W - N_MIX_IN))).astype(BF16),
            "w_gate": w_in[i][:, N_MIX_IN:].astype(BF16),
            "hy_conv_w": hy_conv_w[i], "hy_conv_b": hy_conv_b[i], "hy_skip": hy_skip[i],
            "gla_w_decay": gla_w_decay[i], "gla_b_decay": gla_b_decay[i],
            "ret_gn": ret_gn[i], "gla_gn": gla_gn[i],
            "w_branch": w_branch[i].astype(BF16), "w_out": w_out[i].astype(BF16),
            "ffn_up": ffn_up[i].astype(BF16), "ffn_conv_w": ffn_conv_w[i], "ffn_conv_b": ffn_conv_b[i],
            "ffn_down": ffn_down[i].astype(BF16),
        }
        mod_all = _modulation(c_rows, ada_w[i], ada_b[i])
        filters = {}
        start = 0
        for gi, x in enumerate(groups):
            seq = x.shape[1]
            if seq not in filters:
                filters[seq] = _hyena_filters(seq, flt_w1[i], flt_b1[i], flt_freq[i], flt_w2[i], flt_b2[i],
                                              flt_w3[i], flt_b3[i])
            mod = mod_all[start:start + n_rows[gi]].reshape(n_rows[gi], 6, D_MODEL)
            start += n_rows[gi]
            groups[gi] = _encoder_layer(x, mod, lw, filters[seq])
    return tuple(groups)
```

```python
import functools
import math

import numpy as np
import jax
import jax.numpy as jnp
from jax import lax
from jax.experimental import pallas as pl
from jax.experimental.pallas import tpu as pltpu

F32 = jnp.float32
BF16 = jnp.bfloat16
HIGHEST = lax.Precision.HIGHEST

D_MODEL = 1024
DEPTH = 2
N_BRANCH = 4
MIX_W = 256
N_HEADS = 4
HEAD_V = MIX_W // N_HEADS
RET_DK = 64
GLA_DK = 32
GLA_RANK = 16
GLA_TAU = 16.0
HY_ORDER = 2
FLT_BANDS = 16
FLT_EMB = 1 + 2 * FLT_BANDS
FLT_EMB_PAD = 40
FLT_HIDDEN = 64
D_FF = 2816
GLA_CHUNK = 64
RET_CHUNK = 128
SCAN_MAX_BATCH_BLOCK = 4
SCAN_BLOCK_ROWS = 1024
SCAN_GROUP_ITEMS = 8
ROPE_BASE = 10000.0
EPS = 1e-6
N_MIX_IN = 2848
GLA_IN_W = 896

V7X_VMEM_BYTES = 64 * 1024 * 1024
VMEM_LIMIT = 52 * 1024 * 1024
SUBLANES = 8
LANES = 128
FFT_N2 = 128
INPROJ_ROWS = 1024


def _cparams(n_axes):
    return pltpu.CompilerParams(dimension_semantics=("arbitrary",) * n_axes,
                                vmem_limit_bytes=VMEM_LIMIT)


def _const_spec(shape):
    zeros = (0,) * len(shape)
    return pl.BlockSpec(shape, lambda *_: zeros, pipeline_mode=pl.Buffered(1))


def _dot(a, b):
    return jnp.dot(a, b, preferred_element_type=F32)


def _dot_nt(a, b):
    return lax.dot_general(a, b, (((1,), (1,)), ((), ())), preferred_element_type=F32)


def _dot_tn(a, b):
    return lax.dot_general(a, b, (((0,), (0,)), ((), ())), preferred_element_type=F32)


def _rms(x):
    return x * lax.rsqrt(jnp.mean(x * x, axis=-1, keepdims=True) + EPS)


def _silu(x):
    return x * jax.nn.sigmoid(x)


def _mod_kernel(c_ref, w_ref, b_ref, o_ref):
    c = c_ref[...]
    o_ref[...] = jnp.dot(_silu(c), w_ref[...], precision=HIGHEST, preferred_element_type=F32) + b_ref[...]


def _modulation(c_rows, ada_w, ada_b):
    rows = c_rows.shape[0]
    n_out = ada_w.shape[1]
    tn = D_MODEL
    return pl.pallas_call(
        _mod_kernel,
        grid=(n_out // tn,),
        in_specs=[pl.BlockSpec((rows, D_MODEL), lambda j: (0, 0)),
                  pl.BlockSpec((D_MODEL, tn), lambda j: (0, j)),
                  pl.BlockSpec((1, tn), lambda j: (0, j))],
        out_specs=pl.BlockSpec((rows, tn), lambda j: (0, j)),
        out_shape=jax.ShapeDtypeStruct((rows, n_out), F32),
        compiler_params=_cparams(1),
        name="modulation",
    )(c_rows, ada_w, ada_b.reshape(1, n_out))


def _mod_norm(x, norm_w, scale, shift):
    return _rms(x) * norm_w * (1.0 + scale) + shift


_MIX_SPLITS = ((0, 256), (256, 1280), (1280, 2048), (2048, 2048 + GLA_IN_W))


def _inproj_kernel(x_ref, mod_ref, nw_ref, w_ref, ofn_ref, oret_ref, ohy_ref, ogla_ref):
    h = _mod_norm(x_ref[0], nw_ref[...], mod_ref[0, 1:2, :], mod_ref[0, 0:1, :]).astype(BF16)
    for o_ref, (lo, hi) in zip((ofn_ref, oret_ref, ohy_ref, ogla_ref), _MIX_SPLITS):
        o_ref[0] = _dot(h, w_ref[:, lo:hi])


def _inproj(x, mod, norm_w, w_mix, tm):
    bn, seq, _ = x.shape
    widths = [hi - lo for lo, hi in _MIX_SPLITS]
    return pl.pallas_call(
        _inproj_kernel,
        grid=(bn, seq // tm),
        in_specs=[pl.BlockSpec((1, tm, D_MODEL), lambda b, t: (b, t, 0)),
                  pl.BlockSpec((1, 6, D_MODEL), lambda b, t: (b, 0, 0)),
                  _const_spec((1, D_MODEL)),
                  _const_spec(w_mix.shape)],
        out_specs=[pl.BlockSpec((1, tm, w), lambda b, t: (b, t, 0)) for w in widths],
        out_shape=[jax.ShapeDtypeStruct((bn, seq, w), F32) for w in widths],
        compiler_params=_cparams(2),
        name="inproj",
    )(x, mod, norm_w.reshape(1, D_MODEL), w_mix)


def _halo_specs(tm, seq, width):
    per = tm // SUBLANES
    last = seq // SUBLANES - 1
    prev_spec = pl.BlockSpec((1, SUBLANES, width), lambda b, t: (b, jnp.maximum(t * per - 1, 0), 0))
    next_spec = pl.BlockSpec((1, SUBLANES, width), lambda b, t: (b, jnp.minimum((t + 1) * per, last), 0))
    return prev_spec, next_spec


def _hyconv_kernel(u_ref, up_ref, un_ref, w_ref, b_ref, v_ref, x1_ref, x2_ref):
    t = pl.program_id(1)
    u = u_ref[0]
    tm = u.shape[0]
    prev_row = jnp.where(t > 0, up_ref[0, SUBLANES - 1:SUBLANES, :], 0.0)
    next_row = jnp.where(t < pl.num_programs(1) - 1, un_ref[0, 0:1, :], 0.0)
    rows = lax.broadcasted_iota(jnp.int32, u.shape, 0)
    below = jnp.where(rows == 0, prev_row, pltpu.roll(u, 1, 0))
    above = jnp.where(rows == tm - 1, next_row, pltpu.roll(u, tm - 1, 0))
    y = below * w_ref[0:1, :] + u * w_ref[1:2, :] + above * w_ref[2:3, :] + b_ref[...]
    v_ref[0] = y[:, 0:MIX_W]
    x1_ref[0] = y[:, MIX_W:2 * MIX_W]
    x2_ref[0] = y[:, 2 * MIX_W:3 * MIX_W]


def _hyena_conv3(u, conv_w, conv_b, tm):
    bn, seq, width = u.shape
    prev_spec, next_spec = _halo_specs(tm, seq, width)
    out = jax.ShapeDtypeStruct((bn, seq, MIX_W), F32)
    return pl.pallas_call(
        _hyconv_kernel,
        grid=(bn, seq // tm),
        in_specs=[pl.BlockSpec((1, tm, width), lambda b, t: (b, t, 0)), prev_spec, next_spec,
                  _const_spec((3, width)), _const_spec((1, width))],
        out_specs=[pl.BlockSpec((1, tm, MIX_W), lambda b, t: (b, t, 0))] * 3,
        out_shape=[out, out, out],
        compiler_params=_cparams(2),
        name="hyena_conv3",
    )(u, u, u, conv_w, conv_b.reshape(1, width))


def _dft_outer(n_total, n1, k_rows):
    del n_total
    k = np.arange(n1)[:, None]
    n = np.arange(k_rows)[None, :]
    ang = 2.0 * np.pi * ((k * n) % n1) / n1
    return jnp.asarray(np.concatenate([np.cos(ang), -np.sin(ang)], axis=0), dtype=F32)


def _idft_outer(n1, out_rows):
    n = np.arange(out_rows)[:, None]
    k = np.arange(n1)[None, :]
    ang = 2.0 * np.pi * ((k * n) % n1) / n1
    return jnp.asarray(np.concatenate([np.cos(ang), -np.sin(ang)], axis=1), dtype=F32)


def _dft_inner_tables(n_total, n1):
    n2 = np.arange(FFT_N2)
    ang = 2.0 * np.pi * ((n2[:, None] * n2[None, :]) % FFT_N2) / FFT_N2
    k1 = np.arange(n1)[:, None]
    tw = 2.0 * np.pi * ((k1 * n2[None, :]) % n_total) / n_total
    as32 = lambda a: jnp.asarray(a, dtype=F32)
    return as32(np.cos(ang)), as32(np.sin(ang)), as32(np.cos(tw)), as32(np.sin(tw))


def _lmul_kernel(f_ref, x_ref, o_ref):
    o_ref[0] = _dot(f_ref[...].astype(BF16), x_ref[0].astype(BF16)).astype(o_ref.dtype)


def _fft_outer(x_view, fmat, tn):
    bn, k_rows, width = x_view.shape
    r_rows = fmat.shape[0]
    return pl.pallas_call(
        _lmul_kernel,
        grid=(bn, width // tn),
        in_specs=[_const_spec(fmat.shape),
                  pl.BlockSpec((1, k_rows, tn), lambda b, j: (b, 0, j))],
        out_specs=pl.BlockSpec((1, r_rows, tn), lambda b, j: (b, 0, j)),
        out_shape=jax.ShapeDtypeStruct((bn, r_rows, width), BF16),
        compiler_params=_cparams(2),
        name="fft_outer",
    )(fmat, x_view)


def _build_inner_mats(fc_ref, fs_ref, twc_ref, tws_ref, m_scr, mt_scr, tk1):
    fc = fc_ref[...]
    fs = fs_ref[...]
    for i in range(tk1):
        tc = twc_ref[i:i + 1, :]
        ts = tws_ref[i:i + 1, :]
        ar = fc * tc - fs * ts
        ai = -(fc * ts + fs * tc)
        m_scr[i] = jnp.concatenate(
            [jnp.concatenate([ar, -ai], axis=1), jnp.concatenate([ai, ar], axis=1)], axis=0).astype(BF16)
        if mt_scr is not None:
            art = ar.T
            ait = ai.T
            mt_scr[i] = jnp.concatenate(
                [jnp.concatenate([art, ait], axis=1), jnp.concatenate([-ait, art], axis=1)], axis=0).astype(BF16)


def _inner_forward(m_scr, y_ref, i):
    ycat = jnp.concatenate([y_ref[0, 0, i], y_ref[0, 1, i]], axis=0)
    z = _dot(m_scr[i], ycat)
    return z[:FFT_N2], z[FFT_N2:]


def _spec_mid_kernel(tk1, fc_ref, fs_ref, twc_ref, tws_ref, yf_ref, yb_ref, o_ref, m_scr):
    _build_inner_mats(fc_ref, fs_ref, twc_ref, tws_ref, m_scr, None, tk1)
    for i in range(tk1):
        fr, fi = _inner_forward(m_scr, yf_ref, i)
        br, bi = _inner_forward(m_scr, yb_ref, i)
        o_ref[i, 0] = fr + br
        o_ref[i, 1] = fi - bi


def _conv_mid_kernel(tk1, fc_ref, fs_ref, twc_ref, tws_ref, y_ref, g_ref, o_ref, m_scr, mt_scr):
    @pl.when(pl.program_id(1) == 0)
    def _():
        _build_inner_mats(fc_ref, fs_ref, twc_ref, tws_ref, m_scr, mt_scr, tk1)

    for i in range(tk1):
        zr, zi = _inner_forward(m_scr, y_ref, i)
        gr = g_ref[i, 0]
        gi = g_ref[i, 1]
        pcat = jnp.concatenate([zr * gr - zi * gi, zr * gi + zi * gr], axis=0).astype(BF16)
        q = _dot(mt_scr[i], pcat)
        o_ref[0, 0, i] = q[:FFT_N2].astype(BF16)
        o_ref[0, 1, i] = q[FFT_N2:].astype(BF16)


def _fnet_mid_kernel(tk1, scale, fc_ref, fs_ref, twc_ref, tws_ref, y_ref, cs_ref, o_ref, m_scr):
    @pl.when(pl.program_id(1) == 0)
    def _():
        _build_inner_mats(fc_ref, fs_ref, twc_ref, tws_ref, m_scr, None, tk1)

    for i in range(tk1):
        zr, zi = _inner_forward(m_scr, y_ref, i)
        zcat = jnp.concatenate([zr, zi], axis=1).astype(BF16)
        o_ref[0, :, i * MIX_W:(i + 1) * MIX_W] = _dot(zcat, cs_ref[...].astype(BF16)) * scale


def _inner_table_specs(tk1):
    return [_const_spec((FFT_N2, FFT_N2)), _const_spec((FFT_N2, FFT_N2)),
            pl.BlockSpec((tk1, FFT_N2), lambda k, b: (k, 0)),
            pl.BlockSpec((tk1, FFT_N2), lambda k, b: (k, 0))]


FFT_TK1 = 8


def _filter_spectrum(g_fwd, g_bwd):
    seq, chans = g_fwd.shape
    n_total = 2 * seq
    n1 = n_total // FFT_N2
    half = n1 // 2
    fmat = _dft_outer(n_total, n1, half)
    outer = lambda g: _fft_outer(g.reshape(1, half, FFT_N2 * chans), fmat, 4096).reshape(
        1, 2, n1, FFT_N2, chans)
    tables = _dft_inner_tables(n_total, n1)
    tk1 = FFT_TK1
    y_spec = pl.BlockSpec((1, 2, tk1, FFT_N2, chans), lambda k, b: (0, 0, k, 0, 0))
    return pl.pallas_call(
        functools.partial(_spec_mid_kernel, tk1),
        grid=(n1 // tk1, 1),
        in_specs=_inner_table_specs(tk1) + [y_spec, y_spec],
        out_specs=pl.BlockSpec((tk1, 2, FFT_N2, chans), lambda k, b: (k, 0, 0, 0)),
        out_shape=jax.ShapeDtypeStruct((n1, 2, FFT_N2, chans), F32),
        scratch_shapes=[pltpu.VMEM((tk1, 2 * FFT_N2, 2 * FFT_N2), BF16)],
        compiler_params=_cparams(2),
        name="filter_spectrum",
    )(*tables, outer(g_fwd), outer(g_bwd))


def _hy_out_kernel(n_total, f_ref, q_ref, v_ref, x_ref, gs_ref, sk_ref, o_ref):
    y = _dot(f_ref[...].astype(BF16), q_ref[0])
    inv = 1.0 / ((gs_ref[...] + EPS) * n_total)
    o_ref[0] = x_ref[0] * (y * inv + sk_ref[...] * v_ref[0])


def _long_conv_gate(v, x_gate, spectrum, order, gsum, skip, tn):
    bn, seq, chans = v.shape
    n_total = 2 * seq
    n1 = n_total // FFT_N2
    half = n1 // 2
    width = FFT_N2 * chans
    v_view = v.reshape(bn, half, width)
    y = _fft_outer(v_view, _dft_outer(n_total, n1, half), tn).reshape(bn, 2, n1, FFT_N2, chans)
    tables = _dft_inner_tables(n_total, n1)
    tk1 = FFT_TK1
    q = pl.pallas_call(
        functools.partial(_conv_mid_kernel, tk1),
        grid=(n1 // tk1, bn),
        in_specs=_inner_table_specs(tk1) + [
            pl.BlockSpec((1, 2, tk1, FFT_N2, chans), lambda k, b: (b, 0, k, 0, 0)),
            pl.BlockSpec((tk1, 2, FFT_N2, chans), lambda k, b: (k, 0, 0, order))],
        out_specs=pl.BlockSpec((1, 2, tk1, FFT_N2, chans), lambda k, b: (b, 0, k, 0, 0)),
        out_shape=jax.ShapeDtypeStruct((bn, 2, n1, FFT_N2, chans), BF16),
        scratch_shapes=[pltpu.VMEM((tk1, 2 * FFT_N2, 2 * FFT_N2), BF16),
                        pltpu.VMEM((tk1, 2 * FFT_N2, 2 * FFT_N2), BF16)],
        compiler_params=_cparams(2),
        name="long_conv_mid",
    )(*tables, y, spectrum)
    q_view = q.reshape(bn, 2 * n1, width)
    reps = tn // chans
    gs_row = jnp.tile(gsum.reshape(1, chans), (1, reps))
    sk_row = jnp.tile(skip.reshape(1, chans), (1, reps))
    fmat = _idft_outer(n1, half)
    out = pl.pallas_call(
        functools.partial(_hy_out_kernel, float(n_total)),
        grid=(bn, width // tn),
        in_specs=[_const_spec(fmat.shape),
                  pl.BlockSpec((1, 2 * n1, tn), lambda b, j: (b, 0, j)),
                  pl.BlockSpec((1, half, tn), lambda b, j: (b, 0, j)),
                  pl.BlockSpec((1, half, tn), lambda b, j: (b, 0, j)),
                  _const_spec((1, tn)), _const_spec((1, tn))],
        out_specs=pl.BlockSpec((1, half, tn), lambda b, j: (b, 0, j)),
        out_shape=jax.ShapeDtypeStruct((bn, half, width), F32),
        compiler_params=_cparams(2),
        name="long_conv_out",
    )(fmat, q_view, v_view, x_gate.reshape(bn, half, width), gs_row, sk_row)
    return out.reshape(bn, seq, chans)


def _filter_kernel(feat_ref, w1_ref, b1_ref, fq_ref, w2_ref, b2_ref, w3_ref, b3_ref, dl_ref,
                   gf_ref, gb_ref, s_ref):
    i = pl.program_id(0)
    f = feat_ref[...]
    tm = f.shape[0]
    fq = fq_ref[...]
    hp = functools.partial(jnp.dot, precision=HIGHEST, preferred_element_type=F32)
    h = jnp.sin(fq * (hp(f, w1_ref[...]) + b1_ref[...]))
    h = jnp.sin(fq * (hp(h, w2_ref[...]) + b2_ref[...]))
    filt = hp(h, w3_ref[...]) + b3_ref[...]
    win = jnp.exp(-f[:, 0:1] * dl_ref[...])
    lag = i * tm + lax.broadcasted_iota(jnp.int32, (tm, MIX_W), 0)
    sums = []
    for o in range(HY_ORDER):
        hf = filt[:, (2 * o) * MIX_W:(2 * o + 1) * MIX_W] * win
        hb = jnp.where(lag > 0, filt[:, (2 * o + 1) * MIX_W:(2 * o + 2) * MIX_W] * win, 0.0)
        gf_ref[:, o * MIX_W:(o + 1) * MIX_W] = hf
        gb_ref[:, o * MIX_W:(o + 1) * MIX_W] = hb
        sums.append(jnp.sum(jnp.abs(hf) + jnp.abs(hb), axis=0, keepdims=True))
    total = jnp.concatenate(sums, axis=1)

    @pl.when(i == 0)
    def _():
        s_ref[...] = total

    @pl.when(i > 0)
    def _():
        s_ref[...] = s_ref[...] + total


def _hyena_filter_features(seq):
    t = jnp.linspace(0.0, 1.0, seq, dtype=F32)[:, None]
    w = 2.0 * math.pi * jnp.arange(seq, dtype=F32)[:, None] / seq
    f = jnp.linspace(1e-4, FLT_BANDS - 1, FLT_BANDS, dtype=F32)[None, :]
    feat = jnp.concatenate([t, jnp.cos(f * w), -jnp.sin(f * w)], axis=-1)
    return jnp.pad(feat, ((0, 0), (0, FLT_EMB_PAD - FLT_EMB)))


def _hyena_filters(seq, flt_w1, flt_b1, flt_freq, flt_w2, flt_b2, flt_w3, flt_b3):
    tm = 1024
    feat = _hyena_filter_features(seq)
    deltas = jnp.abs(jnp.linspace(math.log(1e-2) / 0.3, math.log(1e-2) / 1.5, MIX_W, dtype=F32))
    w1 = jnp.pad(flt_w1, ((0, FLT_EMB_PAD - FLT_EMB), (0, 0)))
    n_out = HY_ORDER * 2 * MIX_W
    taps = jax.ShapeDtypeStruct((seq, HY_ORDER * MIX_W), F32)
    g_fwd, g_bwd, gsum = pl.pallas_call(
        _filter_kernel,
        grid=(seq // tm,),
        in_specs=[pl.BlockSpec((tm, FLT_EMB_PAD), lambda i: (i, 0)),
                  _const_spec((FLT_EMB_PAD, FLT_HIDDEN)), _const_spec((1, FLT_HIDDEN)),
                  _const_spec((1, FLT_HIDDEN)),
                  _const_spec((FLT_HIDDEN, FLT_HIDDEN)), _const_spec((1, FLT_HIDDEN)),
                  _const_spec((FLT_HIDDEN, n_out)), _const_spec((1, n_out)),
                  _const_spec((1, MIX_W))],
        out_specs=[pl.BlockSpec((tm, HY_ORDER * MIX_W), lambda i: (i, 0)),
                   pl.BlockSpec((tm, HY_ORDER * MIX_W), lambda i: (i, 0)),
                   pl.BlockSpec((1, HY_ORDER * MIX_W), lambda i: (0, 0))],
        out_shape=[taps, taps, jax.ShapeDtypeStruct((1, HY_ORDER * MIX_W), F32)],
        compiler_params=_cparams(1),
        name="hyena_filter",
    )(feat, w1, flt_b1.reshape(1, -1), flt_freq.reshape(1, -1), flt_w2, flt_b2.reshape(1, -1),
      flt_w3, flt_b3.reshape(1, -1), deltas.reshape(1, MIX_W))
    return _filter_spectrum(g_fwd, g_bwd), gsum.reshape(HY_ORDER, MIX_W)


def _fnet_channel_mats():
    c = np.arange(MIX_W)
    same = (c[:, None] // HEAD_V) == (c[None, :] // HEAD_V)
    ang = 2.0 * np.pi * (((c[:, None] % HEAD_V) * (c[None, :] % HEAD_V)) % HEAD_V) / HEAD_V
    cs = np.concatenate([np.where(same, np.cos(ang), 0.0), np.where(same, np.sin(ang), 0.0)], axis=0)
    return jnp.asarray(cs, dtype=F32)


def _fnet(u):
    bn, seq, chans = u.shape
    n1 = seq // FFT_N2
    width = FFT_N2 * chans
    y = _fft_outer(u.reshape(bn, n1, width), _dft_outer(seq, n1, n1), 4096)
    y = y.reshape(bn, 2, n1, FFT_N2, chans)
    tables = _dft_inner_tables(seq, n1)
    tk1 = FFT_TK1
    scale = 1.0 / math.sqrt(seq * HEAD_V)
    out = pl.pallas_call(
        functools.partial(_fnet_mid_kernel, tk1, scale),
        grid=(n1 // tk1, bn),
        in_specs=_inner_table_specs(tk1) + [
            pl.BlockSpec((1, 2, tk1, FFT_N2, chans), lambda k, b: (b, 0, k, 0, 0)),
            _const_spec((2 * chans, chans))],
        out_specs=pl.BlockSpec((1, FFT_N2, tk1 * chans), lambda k, b: (b, 0, k)),
        out_shape=jax.ShapeDtypeStruct((bn, FFT_N2, n1 * chans), F32),
        scratch_shapes=[pltpu.VMEM((tk1, 2 * FFT_N2, 2 * FFT_N2), BF16)],
        compiler_params=_cparams(2),
        name="fnet_mid",
    )(*tables, y, _fnet_channel_mats())
    return out.reshape(bn, seq, chans)


def _split3(x):
    a = x.astype(BF16)
    r = x - a.astype(F32)
    b = r.astype(BF16)
    c = (r - b.astype(F32)).astype(BF16)
    return a, b, c


def _group_mean_mat():
    c = np.arange(MIX_W)
    same = (c[:, None] // HEAD_V) == (c[None, :] // HEAD_V)
    return jnp.asarray(np.where(same, 1.0 / HEAD_V, 0.0), dtype=BF16)


def _group_mean(x, avg):
    hi = x.astype(BF16)
    lo = (x - hi.astype(F32)).astype(BF16)
    return _dot(hi, avg) + _dot(lo, avg)


def _scan_kernel(cfg, *refs):
    chunk, dk_tot, reverse, is_ret, final = cfg
    refs = list(refs)
    q_ref, k_ref, v_ref = refs[:3]
    pos = 3
    if is_ret:
        cos_ref, sin_ref, lg_ref = refs[pos:pos + 3]
        pos += 3
    else:
        lr_ref, wd_ref, bd_ref = refs[pos:pos + 3]
        pos += 3
    if final:
        ofwd_ref, gate_ref, gain_ref, avg_ref = refs[pos:pos + 4]
        pos += 4
    o_ref, st_ref = refs[pos], refs[pos + 1]

    @pl.when(pl.program_id(1) == 0)
    def _():
        st_ref[...] = jnp.zeros_like(st_ref)

    dk = dk_tot // N_HEADS
    tb = q_ref.shape[1]
    n_chunks = tb // chunk
    wide = N_HEADS * chunk

    def iota(shape, axis):
        return lax.broadcasted_iota(jnp.int32, shape, axis)

    head_k = (iota((wide, dk_tot), 0) // chunk) == (iota((wide, dk_tot), 1) // dk)
    head_v = (iota((wide, MIX_W), 0) // chunk) == (iota((wide, MIX_W), 1) // HEAD_V)
    head_s = (iota((MIX_W, dk_tot), 0) // HEAD_V) == (iota((MIX_W, dk_tot), 1) // dk)
    ri = iota((chunk, wide), 0)
    ci = iota((chunk, wide), 1) % chunk
    keep = (ci > ri) if reverse else (ci <= ri)
    if not is_ret:
        ti = iota((chunk, chunk), 0)
        tj = iota((chunk, chunk), 1)
        tri = jnp.where((tj >= ti) if reverse else (tj <= ti), 1.0, 0.0).astype(BF16)
    if is_ret:
        lane = iota((chunk, MIX_W), 1)
        first_half = (lane % RET_DK) < (RET_DK // 2)
        steps = iota((chunk, dk_tot), 0)
        steps = ((chunk - steps) if reverse else (steps + 1)).astype(F32)

    if is_ret:
        lg = lg_ref[...]
        b_ret = steps * lg
        edge_ret = float(chunk) * lg
        b_mid_ret = float(chunk // 2 + 1) * lg

    bb = q_ref.shape[0]
    group_chunks = max(1, SCAN_GROUP_ITEMS // bb)

    def rotary(x, cos, sin):
        swapped = jnp.where(first_half, pltpu.roll(x, MIX_W - RET_DK // 2, 1),
                            pltpu.roll(x, RET_DK // 2, 1))
        return x * cos + swapped * sin

    def run_group(chunks, states):
        items = [(bi, slice(c * chunk, (c + 1) * chunk)) for c in chunks for bi in range(bb)]
        n = len(items)
        qs_in = [q_ref[bi, rows, :] for bi, rows in items]
        ks_in = [k_ref[bi, rows, :] for bi, rows in items]
        vs = [v_ref[bi, rows, :] for bi, rows in items]
        if is_ret:
            qs_in = [rotary(q, cos_ref[rows, :], sin_ref[rows, :]) for q, (_, rows) in zip(qs_in, items)]
            ks_in = [rotary(k, cos_ref[rows, :], sin_ref[rows, :]) for k, (_, rows) in zip(ks_in, items)]
            bs, edges, mids = [b_ret] * n, [edge_ret] * n, [b_mid_ret] * n
        else:
            lr = jnp.concatenate([lr_ref[bi, rows, :] for bi, rows in items], axis=0)
            z = jnp.dot(lr, wd_ref[...], precision=HIGHEST, preferred_element_type=F32) + bd_ref[...]
            g = (jnp.minimum(z, 0.0) - jnp.log1p(jnp.exp(-jnp.abs(z)))) * (1.0 / GLA_TAU)
            parts = []
            for i in range(n):
                parts.extend(_split3(g[i * chunk:(i + 1) * chunk]))
            cum = _dot(tri, jnp.concatenate(parts, axis=1))
            bs = [cum[:, (3 * i) * dk_tot:(3 * i + 1) * dk_tot] + cum[:, (3 * i + 1) * dk_tot:(3 * i + 2) * dk_tot]
                  + cum[:, (3 * i + 2) * dk_tot:(3 * i + 3) * dk_tot] for i in range(n)]
            edges = [b[0:1, :] if reverse else b[chunk - 1:chunk, :] for b in bs]
            mids = [b[chunk // 2:chunk // 2 + 1, :] for b in bs]
        qs_in = [q * (dk ** -0.5) for q in qs_in]
        q_intra = [(q * jnp.exp(b - m)).astype(BF16) for q, b, m in zip(qs_in, bs, mids)]
        k_intra = [k * jnp.exp(m - b) for k, b, m in zip(ks_in, bs, mids)]
        q_inter = [(q * jnp.exp(b)).astype(BF16) for q, b in zip(qs_in, bs)]
        k_state = [(k * jnp.exp(e - b)).astype(BF16) for k, b, e in zip(ks_in, bs, edges)]
        k_bd = [jnp.where(head_k, jnp.concatenate([k] * N_HEADS, axis=0), 0.0).astype(BF16) for k in k_intra]
        v_bd = [jnp.where(head_v, jnp.concatenate([v] * N_HEADS, axis=0), 0.0).astype(BF16) for v in vs]
        scores = [_dot_nt(q, k) for q, k in zip(q_intra, k_bd)]
        scores = [jnp.where(keep, s, 0.0).astype(BF16) for s in scores]
        intra = [_dot(s, v) for s, v in zip(scores, v_bd)]
        updates = [jnp.where(head_s, _dot_tn(v.astype(BF16), k), 0.0) for v, k in zip(vs, k_state)]
        decays = [jnp.exp(e) for e in edges]
        outs = []
        for i, (bi, _) in enumerate(items):
            outs.append(intra[i] + _dot_nt(q_inter[i], states[bi].astype(BF16)))
            states[bi] = states[bi] * decays[i] + updates[i]
        if final:
            o = jnp.concatenate([o + ofwd_ref[bi, rows, :] for o, (bi, rows) in zip(outs, items)], axis=0)
            gate = jnp.concatenate([gate_ref[bi, rows, :] for bi, rows in items], axis=0)
            avg = avg_ref[...]
            if is_ret:
                o = o - _group_mean(o, avg)
            o = o * lax.rsqrt(_group_mean(o * o, avg) + EPS)
            o = o * gain_ref[...] * _silu(gate)
            outs = [o[i * chunk:(i + 1) * chunk] for i in range(n)]
        for o, (bi, rows) in zip(outs, items):
            o_ref[bi, rows, :] = o

    states = [st_ref[bi] for bi in range(bb)]
    order = list(range(n_chunks - 1, -1, -1) if reverse else range(n_chunks))
    for g0 in range(0, n_chunks, group_chunks):
        run_group(order[g0:g0 + group_chunks], states)
    for bi in range(bb):
        st_ref[bi] = states[bi]


def _scan_call(src, cols, extra, extra_specs, seq_block, chunk, dk_tot, reverse, is_ret, final_args):
    bn, seq, _ = src.shape
    bb = bn if bn <= SCAN_MAX_BATCH_BLOCK else SCAN_MAX_BATCH_BLOCK
    seq_block = min(seq, SCAN_BLOCK_ROWS // bb)
    nb = seq // seq_block
    blk = (lambda t: nb - 1 - t) if reverse else (lambda t: t)
    row_spec = lambda w, col: pl.BlockSpec((bb, seq_block, w), lambda b, t: (b, blk(t), col))
    in_specs = [row_spec(dk_tot, cols[0]), row_spec(dk_tot, cols[1]), row_spec(MIX_W, cols[2])]
    args = [src, src, src]
    for a, s in zip(extra, extra_specs):
        args.append(a)
        in_specs.append(s(blk, bb, seq_block) if callable(s) else s)
    final = final_args is not None
    if final:
        o_fwd, gate_col, gain = final_args
        args += [o_fwd, src, gain.reshape(1, MIX_W), _group_mean_mat()]
        in_specs += [row_spec(MIX_W, 0), row_spec(MIX_W, gate_col), _const_spec((1, MIX_W)),
                     _const_spec((MIX_W, MIX_W))]
    cfg = (chunk, dk_tot, reverse, is_ret, final)
    return pl.pallas_call(
        functools.partial(_scan_kernel, cfg),
        grid=(bn // bb, nb),
        in_specs=in_specs,
        out_specs=pl.BlockSpec((bb, seq_block, MIX_W), lambda b, t: (b, blk(t), 0)),
        out_shape=jax.ShapeDtypeStruct((bn, seq, MIX_W), F32),
        scratch_shapes=[pltpu.VMEM((bb, MIX_W, dk_tot), F32)],
        compiler_params=_cparams(2),
        name=("ret" if is_ret else "gla") + ("_bwd" if reverse else "_fwd"),
    )(*args)


def _rotary_tables(seq):
    half = RET_DK // 2
    inv = ROPE_BASE ** (-jnp.arange(half, dtype=F32) / half)
    ang = jnp.arange(seq, dtype=F32)[:, None] * inv[None, :]
    cos = jnp.cos(ang)
    sin = jnp.sin(ang)
    cos_t = jnp.tile(jnp.concatenate([cos, cos], axis=1), (1, N_HEADS))
    sin_t = jnp.tile(jnp.concatenate([-sin, sin], axis=1), (1, N_HEADS))
    return cos_t, sin_t


def _retention(p_ret, ret_gn, seq_block):
    bn, seq, _ = p_ret.shape
    cos_t, sin_t = _rotary_tables(seq)
    log_gamma = jnp.log(1.0 - 2.0 ** (-5.0 - jnp.arange(N_HEADS, dtype=F32)))
    lg = jnp.repeat(log_gamma, RET_DK).reshape(1, MIX_W)
    extra = [cos_t, sin_t, lg]
    tab = lambda blk, bb, rows: pl.BlockSpec((rows, MIX_W), lambda b, t: (blk(t), 0))
    specs = [tab, tab, _const_spec((1, MIX_W))]
    common = dict(src=p_ret, cols=(0, 1, 2), extra=extra, extra_specs=specs, seq_block=seq_block,
                  chunk=RET_CHUNK, dk_tot=N_HEADS * RET_DK, is_ret=True)
    o_fwd = _scan_call(reverse=False, final_args=None, **common)
    return _scan_call(reverse=True, final_args=(o_fwd, 3, ret_gn), **common)


def _gla(p_gla, w_decay, b_decay, gla_gn, seq_block):
    dk_tot = N_HEADS * GLA_DK
    outs = []
    o_fwd = None
    for direction in range(2):
        wd = jnp.zeros((LANES, dk_tot), F32).at[direction * GLA_RANK:(direction + 1) * GLA_RANK].set(
            w_decay[direction])
        extra = [p_gla, wd, b_decay[direction].reshape(1, dk_tot)]
        lr_spec = lambda blk, bb, rows: pl.BlockSpec((bb, rows, LANES), lambda b, t: (b, blk(t), 6))
        specs = [lr_spec, _const_spec((LANES, dk_tot)), _const_spec((1, dk_tot))]
        final_args = None if direction == 0 else (o_fwd, 2, gla_gn)
        out = _scan_call(src=p_gla, cols=(0, 1, 1), extra=extra, extra_specs=specs, seq_block=seq_block,
                         chunk=GLA_CHUNK, dk_tot=dk_tot, reverse=bool(direction), is_ret=False,
                         final_args=final_args)
        o_fwd = out
        outs.append(out)
    return outs[1]


def _merge_kernel(x_ref, mod_ref, npre_ref, npost_ref, wg_ref, wb_ref, wo_ref,
                  fn_ref, ret_ref, hy_ref, gla_ref, o_ref):
    x = x_ref[0]
    h = _mod_norm(x, npre_ref[...], mod_ref[0, 1:2, :], mod_ref[0, 0:1, :]).astype(BF16)
    merged = None
    for n, br in enumerate((fn_ref, ret_ref, hy_ref, gla_ref)):
        gate = jax.nn.sigmoid(_dot(h, wg_ref[:, n * D_MODEL:(n + 1) * D_MODEL]))
        term = gate * _dot(br[0].astype(BF16), wb_ref[n])
        merged = term if merged is None else merged + term
    y = _dot(merged.astype(BF16), wo_ref[...])
    o_ref[0] = x + mod_ref[0, 2:3, :] * (_rms(y) * npost_ref[...])


def _merge(x, mod, norm_pre, norm_post, w_gate, w_branch, w_out, branches, tm):
    bn, seq, _ = x.shape
    row = lambda w: pl.BlockSpec((1, tm, w), lambda b, t: (b, t, 0))
    return pl.pallas_call(
        _merge_kernel,
        grid=(bn, seq // tm),
        in_specs=[row(D_MODEL), pl.BlockSpec((1, 6, D_MODEL), lambda b, t: (b, 0, 0)),
                  _const_spec((1, D_MODEL)), _const_spec((1, D_MODEL)),
                  _const_spec(w_gate.shape), _const_spec(w_branch.shape), _const_spec(w_out.shape)]
                 + [row(MIX_W)] * N_BRANCH,
        out_specs=row(D_MODEL),
        out_shape=jax.ShapeDtypeStruct(x.shape, F32),
        compiler_params=_cparams(2),
        name="merge",
    )(x, mod, norm_pre.reshape(1, D_MODEL), norm_post.reshape(1, D_MODEL), w_gate, w_branch, w_out,
      *branches)


FFN_COLS = 256


def _ffn_kernel(x_ref, xp_ref, xn_ref, mod_ref, npre_ref, npost_ref, wu_ref, cw_ref, cb_ref, wd_ref, o_ref,
                act_scr):
    t = pl.program_id(1)
    x = x_ref[0]
    tm = x.shape[0]
    ext = tm + 2 * SUBLANES
    xe = jnp.concatenate([xp_ref[0], x, xn_ref[0]], axis=0)
    h = _mod_norm(xe, npre_ref[...], mod_ref[0, 4:5, :], mod_ref[0, 3:4, :])
    rows = lax.broadcasted_iota(jnp.int32, (ext, D_MODEL), 0)
    valid = jnp.logical_and(jnp.logical_or(rows >= SUBLANES, t > 0),
                            jnp.logical_or(rows < tm + SUBLANES, t < pl.num_programs(1) - 1))
    h = jnp.where(valid, h, 0.0).astype(BF16)

    n_tiles = tm // SUBLANES
    sub = lax.broadcasted_iota(jnp.int32, (n_tiles, SUBLANES, FFN_COLS), 1)

    def conv_cols(lo):
        a = _dot(h, wu_ref[:, lo:lo + FFN_COLS]).reshape(n_tiles + 2, SUBLANES, FFN_COLS)
        down = pltpu.roll(a, 1, 1)
        up = pltpu.roll(a, SUBLANES - 1, 1)
        below = jnp.where(sub == 0, down[0:n_tiles], down[1:n_tiles + 1])
        above = jnp.where(sub == SUBLANES - 1, up[2:n_tiles + 2], up[1:n_tiles + 1])
        mid = a[1:n_tiles + 1]
        out = (below * cw_ref[0:1, lo:lo + FFN_COLS] + mid * cw_ref[1:2, lo:lo + FFN_COLS]
               + above * cw_ref[2:3, lo:lo + FFN_COLS] + cb_ref[:, lo:lo + FFN_COLS])
        return out.reshape(tm, FFN_COLS)

    for j in range(D_FF // FFN_COLS):
        gate = conv_cols(j * FFN_COLS)
        val = conv_cols(D_FF + j * FFN_COLS)
        act_scr[:, j * FFN_COLS:(j + 1) * FFN_COLS] = (jax.nn.gelu(gate, approximate=True) * val).astype(BF16)
    y = _dot(act_scr[...], wd_ref[...])
    o_ref[0] = x + mod_ref[0, 5:6, :] * (_rms(y) * npost_ref[...])


def _conv_ffn(x, mod, norm_pre, norm_post, ffn_up, conv_w, conv_b, ffn_down, tm):
    bn, seq, _ = x.shape
    prev_spec, next_spec = _halo_specs(tm, seq, D_MODEL)
    row = pl.BlockSpec((1, tm, D_MODEL), lambda b, t: (b, t, 0))
    return pl.pallas_call(
        _ffn_kernel,
        grid=(bn, seq // tm),
        in_specs=[row, prev_spec, next_spec, pl.BlockSpec((1, 6, D_MODEL), lambda b, t: (b, 0, 0)),
                  _const_spec((1, D_MODEL)), _const_spec((1, D_MODEL)),
                  _const_spec(ffn_up.shape), _const_spec(conv_w.shape), _const_spec((1, 2 * D_FF)),
                  _const_spec(ffn_down.shape)],
        out_specs=row,
        out_shape=jax.ShapeDtypeStruct(x.shape, F32),
        scratch_shapes=[pltpu.VMEM((tm, D_FF), BF16)],
        compiler_params=_cparams(2),
        name="conv_ffn",
    )(x, x, x, mod, norm_pre.reshape(1, D_MODEL), norm_post.reshape(1, D_MODEL), ffn_up, conv_w,
      conv_b.reshape(1, 2 * D_FF), ffn_down)


def _row_tile(seq):
    return min(512, seq)


def _encoder_layer(x, mod, lw, filters):
    seq = x.shape[1]
    tm = _row_tile(seq)
    p_fn, p_ret, p_hy, p_gla = _inproj(x, mod, lw["norm_pre_mix"], lw["w_mix"], min(INPROJ_ROWS, seq))

    o_fn = _fnet(p_fn)
    o_ret = _retention(p_ret, lw["ret_gn"], tm)

    spectrum, gsum = filters
    v, x1, x2 = _hyena_conv3(p_hy, lw["hy_conv_w"], lw["hy_conv_b"], tm)
    tn = min(4096, FFT_N2 * MIX_W)
    z = _long_conv_gate(v, x1, spectrum, 0, gsum[0], lw["hy_skip"][0], tn)
    o_hy = _long_conv_gate(z, x2, spectrum, 1, gsum[1], lw["hy_skip"][1], tn)

    o_gla = _gla(p_gla, lw["gla_w_decay"], lw["gla_b_decay"], lw["gla_gn"], tm)

    x = _merge(x, mod, lw["norm_pre_mix"], lw["norm_post_mix"], lw["w_gate"], lw["w_branch"], lw["w_out"],
               (o_fn, o_ret, o_hy, o_gla), tm)
    return _conv_ffn(x, mod, lw["norm_pre_ffn"], lw["norm_post_ffn"], lw["ffn_up"], lw["ffn_conv_w"],
                     lw["ffn_conv_b"], lw["ffn_down"], tm)


def kernel(x_prompt, x_sample, c_prompt, c_sample, ada_w, ada_b, norm_pre_mix, norm_post_mix, norm_pre_ffn, norm_post_ffn, w_in, hy_conv_w, hy_conv_b, flt_w1, flt_b1, flt_freq, flt_w2, flt_b2, flt_w3, flt_b3, hy_skip, gla_w_decay, gla_b_decay, ret_gn, gla_gn, w_branch, w_out, ffn_up, ffn_conv_w, ffn_conv_b, ffn_down):
    groups = [x_prompt, x_sample]
    n_rows = [c_prompt.shape[0], c_sample.shape[0]]
    c_rows = jnp.concatenate([c_prompt, c_sample], axis=0)
    pad = (-c_rows.shape[0]) % SUBLANES
    c_rows = jnp.pad(c_rows, ((0, pad), (0, 0)))
    for i in range(DEPTH):
        lw = {
            "norm_pre_mix": norm_pre_mix[i], "norm_post_mix": norm_post_mix[i],
            "norm_pre_ffn": norm_pre_ffn[i], "norm_post_ffn": norm_post_ffn[i],
            "w_mix": jnp.pad(w_in[i][:, :N_MIX_IN], ((0, 0), (0, 2048 + GLA_IN_W - N_MIX_IN))).astype(BF16),
            "w_gate": w_in[i][:, N_MIX_IN:].astype(BF16),
            "hy_conv_w": hy_conv_w[i], "hy_conv_b": hy_conv_b[i], "hy_skip": hy_skip[i],
            "gla_w_decay": gla_w_decay[i], "gla_b_decay": gla_b_decay[i],
            "ret_gn": ret_gn[i], "gla_gn": gla_gn[i],
            "w_branch": w_branch[i].astype(BF16), "w_out": w_out[i].astype(BF16),
            "ffn_up": ffn_up[i].astype(BF16), "ffn_conv_w": ffn_conv_w[i], "ffn_conv_b": ffn_conv_b[i],
            "ffn_down": ffn_down[i].astype(BF16),
        }
        mod_all = _modulation(c_rows, ada_w[i], ada_b[i])
        filters = {}
        start = 0
        for gi, x in enumerate(groups):
            seq = x.shape[1]
            if seq not in filters:
                filters[seq] = _hyena_filters(seq, flt_w1[i], flt_b1[i], flt_freq[i], flt_w2[i], flt_b2[i],
                                              flt_w3[i], flt_b3[i])
            mod = mod_all[start:start + n_rows[gi]].reshape(n_rows[gi], 6, D_MODEL)
            start += n_rows[gi]
            groups[gi] = _encoder_layer(x, mod, lw, filters[seq])
    return tuple(groups)
```

```python
import functools
import math

import numpy as np
import jax
import jax.numpy as jnp
from jax import lax
from jax.experimental import pallas as pl
from jax.experimental.pallas import tpu as pltpu

F32 = jnp.float32
BF16 = jnp.bfloat16
HIGHEST = lax.Precision.HIGHEST

D_MODEL = 1024
DEPTH = 2
N_BRANCH = 4
MIX_W = 256
N_HEADS = 4
HEAD_V = MIX_W // N_HEADS
RET_DK = 64
GLA_DK = 32
GLA_RANK = 16
GLA_TAU = 16.0
HY_ORDER = 2
FLT_BANDS = 16
FLT_EMB = 1 + 2 * FLT_BANDS
FLT_EMB_PAD = 40
FLT_HIDDEN = 64
D_FF = 2816
GLA_CHUNK = 64
RET_CHUNK = 128
SCAN_MAX_BATCH_BLOCK = 4
SCAN_BLOCK_ROWS = 1024
SCAN_GROUP_ITEMS = 8
ROPE_BASE = 10000.0
EPS = 1e-6
N_MIX_IN = 2848
GLA_IN_W = 896

V7X_VMEM_BYTES = 64 * 1024 * 1024
VMEM_LIMIT = 52 * 1024 * 1024
SUBLANES = 8
LANES = 128
FFT_N2 = 128
INPROJ_ROWS = 1024


def _cparams(n_axes):
    return pltpu.CompilerParams(dimension_semantics=("arbitrary",) * n_axes,
                                vmem_limit_bytes=VMEM_LIMIT)


def _const_spec(shape):
    zeros = (0,) * len(shape)
    return pl.BlockSpec(shape, lambda *_: zeros, pipeline_mode=pl.Buffered(1))


def _dot(a, b):
    return jnp.dot(a, b, preferred_element_type=F32)


def _dot_nt(a, b):
    return lax.dot_general(a, b, (((1,), (1,)), ((), ())), preferred_element_type=F32)


def _dot_tn(a, b):
    return lax.dot_general(a, b, (((0,), (0,)), ((), ())), preferred_element_type=F32)


def _rms(x):
    return x * lax.rsqrt(jnp.mean(x * x, axis=-1, keepdims=True) + EPS)


def _silu(x):
    return x * jax.nn.sigmoid(x)


def _mod_kernel(c_ref, w_ref, b_ref, o_ref):
    c = c_ref[...]
    o_ref[...] = jnp.dot(_silu(c), w_ref[...], precision=HIGHEST, preferred_element_type=F32) + b_ref[...]


def _modulation(c_rows, ada_w, ada_b):
    rows = c_rows.shape[0]
    n_out = ada_w.shape[1]
    tn = D_MODEL
    return pl.pallas_call(
        _mod_kernel,
        grid=(n_out // tn,),
        in_specs=[pl.BlockSpec((rows, D_MODEL), lambda j: (0, 0)),
                  pl.BlockSpec((D_MODEL, tn), lambda j: (0, j)),
                  pl.BlockSpec((1, tn), lambda j: (0, j))],
        out_specs=pl.BlockSpec((rows, tn), lambda j: (0, j)),
        out_shape=jax.ShapeDtypeStruct((rows, n_out), F32),
        compiler_params=_cparams(1),
        name="modulation",
    )(c_rows, ada_w, ada_b.reshape(1, n_out))


def _mod_norm(x, norm_w, scale, shift):
    return _rms(x) * norm_w * (1.0 + scale) + shift


_MIX_SPLITS = ((0, 256), (256, 1280), (1280, 2048), (2048, 2048 + GLA_IN_W))


def _inproj_kernel(x_ref, mod_ref, nw_ref, w_ref, ofn_ref, oret_ref, ohy_ref, ogla_ref):
    h = _mod_norm(x_ref[0], nw_ref[...], mod_ref[0, 1:2, :], mod_ref[0, 0:1, :]).astype(BF16)
    for o_ref, (lo, hi) in zip((ofn_ref, oret_ref, ohy_ref, ogla_ref), _MIX_SPLITS):
        o_ref[0] = _dot(h, w_ref[:, lo:hi])


def _inproj(x, mod, norm_w, w_mix, tm):
    bn, seq, _ = x.shape
    widths = [hi - lo for lo, hi in _MIX_SPLITS]
    return pl.pallas_call(
        _inproj_kernel,
        grid=(bn, seq // tm),
        in_specs=[pl.BlockSpec((1, tm, D_MODEL), lambda b, t: (b, t, 0)),
                  pl.BlockSpec((1, 6, D_MODEL), lambda b, t: (b, 0, 0)),
                  _const_spec((1, D_MODEL)),
                  _const_spec(w_mix.shape)],
        out_specs=[pl.BlockSpec((1, tm, w), lambda b, t: (b, t, 0)) for w in widths],
        out_shape=[jax.ShapeDtypeStruct((bn, seq, w), F32) for w in widths],
        compiler_params=_cparams(2),
        name="inproj",
    )(x, mod, norm_w.reshape(1, D_MODEL), w_mix)


def _halo_specs(tm, seq, width):
    per = tm // SUBLANES
    last = seq // SUBLANES - 1
    prev_spec = pl.BlockSpec((1, SUBLANES, width), lambda b, t: (b, jnp.maximum(t * per - 1, 0), 0))
    next_spec = pl.BlockSpec((1, SUBLANES, width), lambda b, t: (b, jnp.minimum((t + 1) * per, last), 0))
    return prev_spec, next_spec


def _hyconv_kernel(u_ref, up_ref, un_ref, w_ref, b_ref, v_ref, x1_ref, x2_ref):
    t = pl.program_id(1)
    u = u_ref[0]
    tm = u.shape[0]
    prev_row = jnp.where(t > 0, up_ref[0, SUBLANES - 1:SUBLANES, :], 0.0)
    next_row = jnp.where(t < pl.num_programs(1) - 1, un_ref[0, 0:1, :], 0.0)
    rows = lax.broadcasted_iota(jnp.int32, u.shape, 0)
    below = jnp.where(rows == 0, prev_row, pltpu.roll(u, 1, 0))
    above = jnp.where(rows == tm - 1, next_row, pltpu.roll(u, tm - 1, 0))
    y = below * w_ref[0:1, :] + u * w_ref[1:2, :] + above * w_ref[2:3, :] + b_ref[...]
    v_ref[0] = y[:, 0:MIX_W].astype(v_ref.dtype)
    x1_ref[0] = y[:, MIX_W:2 * MIX_W].astype(x1_ref.dtype)
    x2_ref[0] = y[:, 2 * MIX_W:3 * MIX_W].astype(x2_ref.dtype)


def _hyena_conv3(u, conv_w, conv_b, tm):
    bn, seq, width = u.shape
    prev_spec, next_spec = _halo_specs(tm, seq, width)
    out = jax.ShapeDtypeStruct((bn, seq, MIX_W), BF16)
    return pl.pallas_call(
        _hyconv_kernel,
        grid=(bn, seq // tm),
        in_specs=[pl.BlockSpec((1, tm, width), lambda b, t: (b, t, 0)), prev_spec, next_spec,
                  _const_spec((3, width)), _const_spec((1, width))],
        out_specs=[pl.BlockSpec((1, tm, MIX_W), lambda b, t: (b, t, 0))] * 3,
        out_shape=[out, out, out],
        compiler_params=_cparams(2),
        name="hyena_conv3",
    )(u, u, u, conv_w, conv_b.reshape(1, width))


def _dft_outer(n_total, n1, k_rows):
    del n_total
    k = np.arange(n1)[:, None]
    n = np.arange(k_rows)[None, :]
    ang = 2.0 * np.pi * ((k * n) % n1) / n1
    return jnp.asarray(np.concatenate([np.cos(ang), -np.sin(ang)], axis=0), dtype=F32)


def _idft_outer(n1, out_rows):
    n = np.arange(out_rows)[:, None]
    k = np.arange(n1)[None, :]
    ang = 2.0 * np.pi * ((k * n) % n1) / n1
    return jnp.asarray(np.concatenate([np.cos(ang), -np.sin(ang)], axis=1), dtype=F32)


def _dft_inner_tables(n_total, n1):
    n2 = np.arange(FFT_N2)
    ang = 2.0 * np.pi * ((n2[:, None] * n2[None, :]) % FFT_N2) / FFT_N2
    k1 = np.arange(n1)[:, None]
    tw = 2.0 * np.pi * ((k1 * n2[None, :]) % n_total) / n_total
    as32 = lambda a: jnp.asarray(a, dtype=F32)
    return as32(np.cos(ang)), as32(np.sin(ang)), as32(np.cos(tw)), as32(np.sin(tw))


def _lmul_kernel(f_ref, x_ref, o_ref):
    o_ref[0] = _dot(f_ref[...].astype(BF16), x_ref[0].astype(BF16)).astype(o_ref.dtype)


def _fft_outer(x_view, fmat, tn):
    bn, k_rows, width = x_view.shape
    r_rows = fmat.shape[0]
    return pl.pallas_call(
        _lmul_kernel,
        grid=(bn, width // tn),
        in_specs=[_const_spec(fmat.shape),
                  pl.BlockSpec((1, k_rows, tn), lambda b, j: (b, 0, j))],
        out_specs=pl.BlockSpec((1, r_rows, tn), lambda b, j: (b, 0, j)),
        out_shape=jax.ShapeDtypeStruct((bn, r_rows, width), BF16),
        compiler_params=_cparams(2),
        name="fft_outer",
    )(fmat, x_view)


def _build_inner_mats(fc_ref, fs_ref, twc_ref, tws_ref, m_scr, mt_scr, tk1):
    fc = fc_ref[...]
    fs = fs_ref[...]
    for i in range(tk1):
        tc = twc_ref[i:i + 1, :]
        ts = tws_ref[i:i + 1, :]
        ar = fc * tc - fs * ts
        ai = -(fc * ts + fs * tc)
        m_scr[i] = jnp.concatenate(
            [jnp.concatenate([ar, -ai], axis=1), jnp.concatenate([ai, ar], axis=1)], axis=0).astype(BF16)
        if mt_scr is not None:
            art = ar.T
            ait = ai.T
            mt_scr[i] = jnp.concatenate(
                [jnp.concatenate([art, ait], axis=1), jnp.concatenate([-ait, art], axis=1)], axis=0).astype(BF16)


def _inner_forward(m_scr, y_ref, i):
    ycat = jnp.concatenate([y_ref[0, 0, i], y_ref[0, 1, i]], axis=0)
    z = _dot(m_scr[i], ycat)
    return z[:FFT_N2], z[FFT_N2:]


def _spec_mid_kernel(tk1, fc_ref, fs_ref, twc_ref, tws_ref, yf_ref, yb_ref, o_ref, m_scr):
    _build_inner_mats(fc_ref, fs_ref, twc_ref, tws_ref, m_scr, None, tk1)
    for i in range(tk1):
        fr, fi = _inner_forward(m_scr, yf_ref, i)
        br, bi = _inner_forward(m_scr, yb_ref, i)
        o_ref[i, 0] = fr + br
        o_ref[i, 1] = fi - bi


def _conv_mid_kernel(tk1, fc_ref, fs_ref, twc_ref, tws_ref, y_ref, g_ref, o_ref, m_scr, mt_scr):
    @pl.when(pl.program_id(1) == 0)
    def _():
        _build_inner_mats(fc_ref, fs_ref, twc_ref, tws_ref, m_scr, mt_scr, tk1)

    for i in range(tk1):
        zr, zi = _inner_forward(m_scr, y_ref, i)
        gr = g_ref[i, 0]
        gi = g_ref[i, 1]
        pcat = jnp.concatenate([zr * gr - zi * gi, zr * gi + zi * gr], axis=0).astype(BF16)
        q = _dot(mt_scr[i], pcat)
        o_ref[0, 0, i] = q[:FFT_N2].astype(BF16)
        o_ref[0, 1, i] = q[FFT_N2:].astype(BF16)


def _fnet_mid_kernel(tk1, scale, fc_ref, fs_ref, twc_ref, tws_ref, y_ref, cs_ref, o_ref, m_scr):
    @pl.when(pl.program_id(1) == 0)
    def _():
        _build_inner_mats(fc_ref, fs_ref, twc_ref, tws_ref, m_scr, None, tk1)

    for i in range(tk1):
        zr, zi = _inner_forward(m_scr, y_ref, i)
        zcat = jnp.concatenate([zr, zi], axis=1).astype(BF16)
        o_ref[0, :, i * MIX_W:(i + 1) * MIX_W] = (_dot(zcat, cs_ref[...].astype(BF16)) * scale).astype(o_ref.dtype)


def _inner_table_specs(tk1):
    return [_const_spec((FFT_N2, FFT_N2)), _const_spec((FFT_N2, FFT_N2)),
            pl.BlockSpec((tk1, FFT_N2), lambda k, b: (k, 0)),
            pl.BlockSpec((tk1, FFT_N2), lambda k, b: (k, 0))]


FFT_TK1 = 8


def _filter_spectrum(g_fwd, g_bwd):
    seq, chans = g_fwd.shape
    n_total = 2 * seq
    n1 = n_total // FFT_N2
    half = n1 // 2
    fmat = _dft_outer(n_total, n1, half)
    outer = lambda g: _fft_outer(g.reshape(1, half, FFT_N2 * chans), fmat, 4096).reshape(
        1, 2, n1, FFT_N2, chans)
    tables = _dft_inner_tables(n_total, n1)
    tk1 = FFT_TK1
    y_spec = pl.BlockSpec((1, 2, tk1, FFT_N2, chans), lambda k, b: (0, 0, k, 0, 0))
    return pl.pallas_call(
        functools.partial(_spec_mid_kernel, tk1),
        grid=(n1 // tk1, 1),
        in_specs=_inner_table_specs(tk1) + [y_spec, y_spec],
        out_specs=pl.BlockSpec((tk1, 2, FFT_N2, chans), lambda k, b: (k, 0, 0, 0)),
        out_shape=jax.ShapeDtypeStruct((n1, 2, FFT_N2, chans), F32),
        scratch_shapes=[pltpu.VMEM((tk1, 2 * FFT_N2, 2 * FFT_N2), BF16)],
        compiler_params=_cparams(2),
        name="filter_spectrum",
    )(*tables, outer(g_fwd), outer(g_bwd))


def _hy_out_kernel(n_total, f_ref, q_ref, v_ref, x_ref, gs_ref, sk_ref, o_ref):
    y = _dot(f_ref[...].astype(BF16), q_ref[0])
    inv = 1.0 / ((gs_ref[...] + EPS) * n_total)
    o_ref[0] = (x_ref[0] * (y * inv + sk_ref[...] * v_ref[0])).astype(o_ref.dtype)


def _long_conv_gate(v, x_gate, spectrum, order, gsum, skip, tn):
    bn, seq, chans = v.shape
    n_total = 2 * seq
    n1 = n_total // FFT_N2
    half = n1 // 2
    width = FFT_N2 * chans
    v_view = v.reshape(bn, half, width)
    y = _fft_outer(v_view, _dft_outer(n_total, n1, half), tn).reshape(bn, 2, n1, FFT_N2, chans)
    tables = _dft_inner_tables(n_total, n1)
    tk1 = FFT_TK1
    q = pl.pallas_call(
        functools.partial(_conv_mid_kernel, tk1),
        grid=(n1 // tk1, bn),
        in_specs=_inner_table_specs(tk1) + [
            pl.BlockSpec((1, 2, tk1, FFT_N2, chans), lambda k, b: (b, 0, k, 0, 0)),
            pl.BlockSpec((tk1, 2, FFT_N2, chans), lambda k, b: (k, 0, 0, order))],
        out_specs=pl.BlockSpec((1, 2, tk1, FFT_N2, chans), lambda k, b: (b, 0, k, 0, 0)),
        out_shape=jax.ShapeDtypeStruct((bn, 2, n1, FFT_N2, chans), BF16),
        scratch_shapes=[pltpu.VMEM((tk1, 2 * FFT_N2, 2 * FFT_N2), BF16),
                        pltpu.VMEM((tk1, 2 * FFT_N2, 2 * FFT_N2), BF16)],
        compiler_params=_cparams(2),
        name="long_conv_mid",
    )(*tables, y, spectrum)
    q_view = q.reshape(bn, 2 * n1, width)
    reps = tn // chans
    gs_row = jnp.tile(gsum.reshape(1, chans), (1, reps))
    sk_row = jnp.tile(skip.reshape(1, chans), (1, reps))
    fmat = _idft_outer(n1, half)
    out = pl.pallas_call(
        functools.partial(_hy_out_kernel, float(n_total)),
        grid=(bn, width // tn),
        in_specs=[_const_spec(fmat.shape),
                  pl.BlockSpec((1, 2 * n1, tn), lambda b, j: (b, 0, j)),
                  pl.BlockSpec((1, half, tn), lambda b, j: (b, 0, j)),
                  pl.BlockSpec((1, half, tn), lambda b, j: (b, 0, j)),
                  _const_spec((1, tn)), _const_spec((1, tn))],
        out_specs=pl.BlockSpec((1, half, tn), lambda b, j: (b, 0, j)),
        out_shape=jax.ShapeDtypeStruct((bn, half, width), BF16),
        compiler_params=_cparams(2),
        name="long_conv_out",
    )(fmat, q_view, v_view, x_gate.reshape(bn, half, width), gs_row, sk_row)
    return out.reshape(bn, seq, chans)


def _filter_kernel(feat_ref, w1_ref, b1_ref, fq_ref, w2_ref, b2_ref, w3_ref, b3_ref, dl_ref,
                   gf_ref, gb_ref, s_ref):
    i = pl.program_id(0)
    f = feat_ref[...]
    tm = f.shape[0]
    fq = fq_ref[...]
    hp = functools.partial(jnp.dot, precision=HIGHEST, preferred_element_type=F32)
    h = jnp.sin(fq * (hp(f, w1_ref[...]) + b1_ref[...]))
    h = jnp.sin(fq * (hp(h, w2_ref[...]) + b2_ref[...]))
    filt = hp(h, w3_ref[...]) + b3_ref[...]
    win = jnp.exp(-f[:, 0:1] * dl_ref[...])
    lag = i * tm + lax.broadcasted_iota(jnp.int32, (tm, MIX_W), 0)
    sums = []
    for o in range(HY_ORDER):
        hf = filt[:, (2 * o) * MIX_W:(2 * o + 1) * MIX_W] * win
        hb = jnp.where(lag > 0, filt[:, (2 * o + 1) * MIX_W:(2 * o + 2) * MIX_W] * win, 0.0)
        gf_ref[:, o * MIX_W:(o + 1) * MIX_W] = hf
        gb_ref[:, o * MIX_W:(o + 1) * MIX_W] = hb
        sums.append(jnp.sum(jnp.abs(hf) + jnp.abs(hb), axis=0, keepdims=True))
    total = jnp.concatenate(sums, axis=1)

    @pl.when(i == 0)
    def _():
        s_ref[...] = total

    @pl.when(i > 0)
    def _():
        s_ref[...] = s_ref[...] + total


def _hyena_filter_features(seq):
    t = jnp.linspace(0.0, 1.0, seq, dtype=F32)[:, None]
    w = 2.0 * math.pi * jnp.arange(seq, dtype=F32)[:, None] / seq
    f = jnp.linspace(1e-4, FLT_BANDS - 1, FLT_BANDS, dtype=F32)[None, :]
    feat = jnp.concatenate([t, jnp.cos(f * w), -jnp.sin(f * w)], axis=-1)
    return jnp.pad(feat, ((0, 0), (0, FLT_EMB_PAD - FLT_EMB)))


def _hyena_filters(seq, flt_w1, flt_b1, flt_freq, flt_w2, flt_b2, flt_w3, flt_b3):
    tm = 1024
    feat = _hyena_filter_features(seq)
    deltas = jnp.abs(jnp.linspace(math.log(1e-2) / 0.3, math.log(1e-2) / 1.5, MIX_W, dtype=F32))
    w1 = jnp.pad(flt_w1, ((0, FLT_EMB_PAD - FLT_EMB), (0, 0)))
    n_out = HY_ORDER * 2 * MIX_W
    taps = jax.ShapeDtypeStruct((seq, HY_ORDER * MIX_W), F32)
    g_fwd, g_bwd, gsum = pl.pallas_call(
        _filter_kernel,
        grid=(seq // tm,),
        in_specs=[pl.BlockSpec((tm, FLT_EMB_PAD), lambda i: (i, 0)),
                  _const_spec((FLT_EMB_PAD, FLT_HIDDEN)), _const_spec((1, FLT_HIDDEN)),
                  _const_spec((1, FLT_HIDDEN)),
                  _const_spec((FLT_HIDDEN, FLT_HIDDEN)), _const_spec((1, FLT_HIDDEN)),
                  _const_spec((FLT_HIDDEN, n_out)), _const_spec((1, n_out)),
                  _const_spec((1, MIX_W))],
        out_specs=[pl.BlockSpec((tm, HY_ORDER * MIX_W), lambda i: (i, 0)),
                   pl.BlockSpec((tm, HY_ORDER * MIX_W), lambda i: (i, 0)),
                   pl.BlockSpec((1, HY_ORDER * MIX_W), lambda i: (0, 0))],
        out_shape=[taps, taps, jax.ShapeDtypeStruct((1, HY_ORDER * MIX_W), F32)],
        compiler_params=_cparams(1),
        name="hyena_filter",
    )(feat, w1, flt_b1.reshape(1, -1), flt_freq.reshape(1, -1), flt_w2, flt_b2.reshape(1, -1),
      flt_w3, flt_b3.reshape(1, -1), deltas.reshape(1, MIX_W))
    return _filter_spectrum(g_fwd, g_bwd), gsum.reshape(HY_ORDER, MIX_W)


def _fnet_channel_mats():
    c = np.arange(MIX_W)
    same = (c[:, None] // HEAD_V) == (c[None, :] // HEAD_V)
    ang = 2.0 * np.pi * (((c[:, None] % HEAD_V) * (c[None, :] % HEAD_V)) % HEAD_V) / HEAD_V
    cs = np.concatenate([np.where(same, np.cos(ang), 0.0), np.where(same, np.sin(ang), 0.0)], axis=0)
    return jnp.asarray(cs, dtype=F32)


def _fnet(u):
    bn, seq, chans = u.shape
    n1 = seq // FFT_N2
    width = FFT_N2 * chans
    y = _fft_outer(u.reshape(bn, n1, width), _dft_outer(seq, n1, n1), 4096)
    y = y.reshape(bn, 2, n1, FFT_N2, chans)
    tables = _dft_inner_tables(seq, n1)
    tk1 = FFT_TK1
    scale = 1.0 / math.sqrt(seq * HEAD_V)
    out = pl.pallas_call(
        functools.partial(_fnet_mid_kernel, tk1, scale),
        grid=(n1 // tk1, bn),
        in_specs=_inner_table_specs(tk1) + [
            pl.BlockSpec((1, 2, tk1, FFT_N2, chans), lambda k, b: (b, 0, k, 0, 0)),
            _const_spec((2 * chans, chans))],
        out_specs=pl.BlockSpec((1, FFT_N2, tk1 * chans), lambda k, b: (b, 0, k)),
        out_shape=jax.ShapeDtypeStruct((bn, FFT_N2, n1 * chans), BF16),
        scratch_shapes=[pltpu.VMEM((tk1, 2 * FFT_N2, 2 * FFT_N2), BF16)],
        compiler_params=_cparams(2),
        name="fnet_mid",
    )(*tables, y, _fnet_channel_mats())
    return out.reshape(bn, seq, chans)


def _split3(x):
    a = x.astype(BF16)
    r = x - a.astype(F32)
    b = r.astype(BF16)
    c = (r - b.astype(F32)).astype(BF16)
    return a, b, c


def _group_mean_mat():
    c = np.arange(MIX_W)
    same = (c[:, None] // HEAD_V) == (c[None, :] // HEAD_V)
    return jnp.asarray(np.where(same, 1.0 / HEAD_V, 0.0), dtype=BF16)


def _group_mean(x, avg):
    hi = x.astype(BF16)
    lo = (x - hi.astype(F32)).astype(BF16)
    return _dot(hi, avg) + _dot(lo, avg)


def _scan_kernel(cfg, *refs):
    chunk, dk_tot, reverse, is_ret, final = cfg
    refs = list(refs)
    q_ref, k_ref, v_ref = refs[:3]
    pos = 3
    if is_ret:
        cos_ref, sin_ref, lg_ref = refs[pos:pos + 3]
        pos += 3
    else:
        lr_ref, wd_ref, bd_ref = refs[pos:pos + 3]
        pos += 3
    if final:
        ofwd_ref, gate_ref, gain_ref, avg_ref = refs[pos:pos + 4]
        pos += 4
    o_ref, st_ref = refs[pos], refs[pos + 1]

    @pl.when(pl.program_id(1) == 0)
    def _():
        st_ref[...] = jnp.zeros_like(st_ref)

    dk = dk_tot // N_HEADS
    tb = q_ref.shape[1]
    n_chunks = tb // chunk
    wide = N_HEADS * chunk

    def iota(shape, axis):
        return lax.broadcasted_iota(jnp.int32, shape, axis)

    head_k = (iota((wide, dk_tot), 0) // chunk) == (iota((wide, dk_tot), 1) // dk)
    head_v = (iota((wide, MIX_W), 0) // chunk) == (iota((wide, MIX_W), 1) // HEAD_V)
    head_s = (iota((MIX_W, dk_tot), 0) // HEAD_V) == (iota((MIX_W, dk_tot), 1) // dk)
    ri = iota((chunk, wide), 0)
    ci = iota((chunk, wide), 1) % chunk
    keep = (ci > ri) if reverse else (ci <= ri)
    if not is_ret:
        ti = iota((chunk, chunk), 0)
        tj = iota((chunk, chunk), 1)
        tri = jnp.where((tj >= ti) if reverse else (tj <= ti), 1.0, 0.0).astype(BF16)
    if is_ret:
        lane = iota((chunk, MIX_W), 1)
        first_half = (lane % RET_DK) < (RET_DK // 2)
        steps = iota((chunk, dk_tot), 0)
        steps = ((chunk - steps) if reverse else (steps + 1)).astype(F32)

    if is_ret:
        lg = lg_ref[...]
        b_ret = steps * lg
        edge_ret = float(chunk) * lg
        b_mid_ret = float(chunk // 2 + 1) * lg

    bb = q_ref.shape[0]
    group_chunks = max(1, SCAN_GROUP_ITEMS // bb)

    def rotary(x, cos, sin):
        swapped = jnp.where(first_half, pltpu.roll(x, MIX_W - RET_DK // 2, 1),
                            pltpu.roll(x, RET_DK // 2, 1))
        return x * cos + swapped * sin

    def run_group(chunks, states):
        items = [(bi, slice(c * chunk, (c + 1) * chunk)) for c in chunks for bi in range(bb)]
        n = len(items)
        qs_in = [q_ref[bi, rows, :] for bi, rows in items]
        ks_in = [k_ref[bi, rows, :] for bi, rows in items]
        vs = [v_ref[bi, rows, :] for bi, rows in items]
        if is_ret:
            qs_in = [rotary(q, cos_ref[rows, :], sin_ref[rows, :]) for q, (_, rows) in zip(qs_in, items)]
            ks_in = [rotary(k, cos_ref[rows, :], sin_ref[rows, :]) for k, (_, rows) in zip(ks_in, items)]
            bs, edges, mids = [b_ret] * n, [edge_ret] * n, [b_mid_ret] * n
        else:
            lr = jnp.concatenate([lr_ref[bi, rows, :] for bi, rows in items], axis=0)
            z = jnp.dot(lr, wd_ref[...], precision=HIGHEST, preferred_element_type=F32) + bd_ref[...]
            g = (jnp.minimum(z, 0.0) - jnp.log1p(jnp.exp(-jnp.abs(z)))) * (1.0 / GLA_TAU)
            parts = []
            for i in range(n):
                parts.extend(_split3(g[i * chunk:(i + 1) * chunk]))
            cum = _dot(tri, jnp.concatenate(parts, axis=1))
            bs = [cum[:, (3 * i) * dk_tot:(3 * i + 1) * dk_tot] + cum[:, (3 * i + 1) * dk_tot:(3 * i + 2) * dk_tot]
                  + cum[:, (3 * i + 2) * dk_tot:(3 * i + 3) * dk_tot] for i in range(n)]
            edges = [b[0:1, :] if reverse else b[chunk - 1:chunk, :] for b in bs]
            mids = [b[chunk // 2:chunk // 2 + 1, :] for b in bs]
        qs_in = [q * (dk ** -0.5) for q in qs_in]
        q_intra = [(q * jnp.exp(b - m)).astype(BF16) for q, b, m in zip(qs_in, bs, mids)]
        k_intra = [k * jnp.exp(m - b) for k, b, m in zip(ks_in, bs, mids)]
        q_inter = [(q * jnp.exp(b)).astype(BF16) for q, b in zip(qs_in, bs)]
        k_state = [(k * jnp.exp(e - b)).astype(BF16) for k, b, e in zip(ks_in, bs, edges)]
        k_bd = [jnp.where(head_k, jnp.concatenate([k] * N_HEADS, axis=0), 0.0).astype(BF16) for k in k_intra]
        v_bd = [jnp.where(head_v, jnp.concatenate([v] * N_HEADS, axis=0), 0.0).astype(BF16) for v in vs]
        scores = [_dot_nt(q, k) for q, k in zip(q_intra, k_bd)]
        scores = [jnp.where(keep, s, 0.0).astype(BF16) for s in scores]
        intra = [_dot(s, v) for s, v in zip(scores, v_bd)]
        updates = [jnp.where(head_s, _dot_tn(v.astype(BF16), k), 0.0) for v, k in zip(vs, k_state)]
        decays = [jnp.exp(e) for e in edges]
        outs = []
        for i, (bi, _) in enumerate(items):
            outs.append(intra[i] + _dot_nt(q_inter[i], states[bi].astype(BF16)))
            states[bi] = states[bi] * decays[i] + updates[i]
        if final:
            o = jnp.concatenate([o + ofwd_ref[bi, rows, :] for o, (bi, rows) in zip(outs, items)], axis=0)
            gate = jnp.concatenate([gate_ref[bi, rows, :] for bi, rows in items], axis=0)
            avg = avg_ref[...]
            if is_ret:
                o = o - _group_mean(o, avg)
            o = o * lax.rsqrt(_group_mean(o * o, avg) + EPS)
            o = o * gain_ref[...] * _silu(gate)
            outs = [o[i * chunk:(i + 1) * chunk] for i in range(n)]
        for o, (bi, rows) in zip(outs, items):
            o_ref[bi, rows, :] = o.astype(o_ref.dtype)

    states = [st_ref[bi] for bi in range(bb)]
    order = list(range(n_chunks - 1, -1, -1) if reverse else range(n_chunks))
    for g0 in range(0, n_chunks, group_chunks):
        run_group(order[g0:g0 + group_chunks], states)
    for bi in range(bb):
        st_ref[bi] = states[bi]


def _scan_call(src, cols, extra, extra_specs, seq_block, chunk, dk_tot, reverse, is_ret, final_args):
    bn, seq, _ = src.shape
    bb = bn if bn <= SCAN_MAX_BATCH_BLOCK else SCAN_MAX_BATCH_BLOCK
    seq_block = min(seq, SCAN_BLOCK_ROWS // bb)
    nb = seq // seq_block
    blk = (lambda t: nb - 1 - t) if reverse else (lambda t: t)
    row_spec = lambda w, col: pl.BlockSpec((bb, seq_block, w), lambda b, t: (b, blk(t), col))
    in_specs = [row_spec(dk_tot, cols[0]), row_spec(dk_tot, cols[1]), row_spec(MIX_W, cols[2])]
    args = [src, src, src]
    for a, s in zip(extra, extra_specs):
        args.append(a)
        in_specs.append(s(blk, bb, seq_block) if callable(s) else s)
    final = final_args is not None
    if final:
        o_fwd, gate_col, gain = final_args
        args += [o_fwd, src, gain.reshape(1, MIX_W), _group_mean_mat()]
        in_specs += [row_spec(MIX_W, 0), row_spec(MIX_W, gate_col), _const_spec((1, MIX_W)),
                     _const_spec((MIX_W, MIX_W))]
    cfg = (chunk, dk_tot, reverse, is_ret, final)
    return pl.pallas_call(
        functools.partial(_scan_kernel, cfg),
        grid=(bn // bb, nb),
        in_specs=in_specs,
        out_specs=pl.BlockSpec((bb, seq_block, MIX_W), lambda b, t: (b, blk(t), 0)),
        out_shape=jax.ShapeDtypeStruct((bn, seq, MIX_W), BF16 if final else F32),
        scratch_shapes=[pltpu.VMEM((bb, MIX_W, dk_tot), F32)],
        compiler_params=_cparams(2),
        name=("ret" if is_ret else "gla") + ("_bwd" if reverse else "_fwd"),
    )(*args)


def _rotary_tables(seq):
    half = RET_DK // 2
    inv = ROPE_BASE ** (-jnp.arange(half, dtype=F32) / half)
    ang = jnp.arange(seq, dtype=F32)[:, None] * inv[None, :]
    cos = jnp.cos(ang)
    sin = jnp.sin(ang)
    cos_t = jnp.tile(jnp.concatenate([cos, cos], axis=1), (1, N_HEADS))
    sin_t = jnp.tile(jnp.concatenate([-sin, sin], axis=1), (1, N_HEADS))
    return cos_t, sin_t


def _retention(p_ret, ret_gn, seq_block):
    bn, seq, _ = p_ret.shape
    cos_t, sin_t = _rotary_tables(seq)
    log_gamma = jnp.log(1.0 - 2.0 ** (-5.0 - jnp.arange(N_HEADS, dtype=F32)))
    lg = jnp.repeat(log_gamma, RET_DK).reshape(1, MIX_W)
    extra = [cos_t, sin_t, lg]
    tab = lambda blk, bb, rows: pl.BlockSpec((rows, MIX_W), lambda b, t: (blk(t), 0))
    specs = [tab, tab, _const_spec((1, MIX_W))]
    common = dict(src=p_ret, cols=(0, 1, 2), extra=extra, extra_specs=specs, seq_block=seq_block,
                  chunk=RET_CHUNK, dk_tot=N_HEADS * RET_DK, is_ret=True)
    o_fwd = _scan_call(reverse=False, final_args=None, **common)
    return _scan_call(reverse=True, final_args=(o_fwd, 3, ret_gn), **common)


def _gla(p_gla, w_decay, b_decay, gla_gn, seq_block):
    dk_tot = N_HEADS * GLA_DK
    outs = []
    o_fwd = None
    for direction in range(2):
        wd = jnp.zeros((LANES, dk_tot), F32).at[direction * GLA_RANK:(direction + 1) * GLA_RANK].set(
            w_decay[direction])
        extra = [p_gla, wd, b_decay[direction].reshape(1, dk_tot)]
        lr_spec = lambda blk, bb, rows: pl.BlockSpec((bb, rows, LANES), lambda b, t: (b, blk(t), 6))
        specs = [lr_spec, _const_spec((LANES, dk_tot)), _const_spec((1, dk_tot))]
        final_args = None if direction == 0 else (o_fwd, 2, gla_gn)
        out = _scan_call(src=p_gla, cols=(0, 1, 1), extra=extra, extra_specs=specs, seq_block=seq_block,
                         chunk=GLA_CHUNK, dk_tot=dk_tot, reverse=bool(direction), is_ret=False,
                         final_args=final_args)
        o_fwd = out
        outs.append(out)
    return outs[1]


def _merge_kernel(x_ref, mod_ref, npre_ref, npost_ref, wg_ref, wb_ref, wo_ref,
                  fn_ref, ret_ref, hy_ref, gla_ref, o_ref):
    x = x_ref[0]
    h = _mod_norm(x, npre_ref[...], mod_ref[0, 1:2, :], mod_ref[0, 0:1, :]).astype(BF16)
    merged = None
    for n, br in enumerate((fn_ref, ret_ref, hy_ref, gla_ref)):
        gate = jax.nn.sigmoid(_dot(h, wg_ref[:, n * D_MODEL:(n + 1) * D_MODEL]))
        term = gate * _dot(br[0].astype(BF16), wb_ref[n])
        merged = term if merged is None else merged + term
    y = _dot(merged.astype(BF16), wo_ref[...])
    o_ref[0] = x + mod_ref[0, 2:3, :] * (_rms(y) * npost_ref[...])


def _merge(x, mod, norm_pre, norm_post, w_gate, w_branch, w_out, branches, tm):
    bn, seq, _ = x.shape
    row = lambda w: pl.BlockSpec((1, tm, w), lambda b, t: (b, t, 0))
    return pl.pallas_call(
        _merge_kernel,
        grid=(bn, seq // tm),
        in_specs=[row(D_MODEL), pl.BlockSpec((1, 6, D_MODEL), lambda b, t: (b, 0, 0)),
                  _const_spec((1, D_MODEL)), _const_spec((1, D_MODEL)),
                  _const_spec(w_gate.shape), _const_spec(w_branch.shape), _const_spec(w_out.shape)]
                 + [row(MIX_W)] * N_BRANCH,
        out_specs=row(D_MODEL),
        out_shape=jax.ShapeDtypeStruct(x.shape, F32),
        compiler_params=_cparams(2),
        name="merge",
    )(x, mod, norm_pre.reshape(1, D_MODEL), norm_post.reshape(1, D_MODEL), w_gate, w_branch, w_out,
      *branches)


FFN_COLS = 256


def _ffn_kernel(x_ref, xp_ref, xn_ref, mod_ref, npre_ref, npost_ref, wu_ref, cw_ref, cb_ref, wd_ref, o_ref,
                act_scr):
    t = pl.program_id(1)
    x = x_ref[0]
    tm = x.shape[0]
    ext = tm + 2 * SUBLANES
    xe = jnp.concatenate([xp_ref[0], x, xn_ref[0]], axis=0)
    h = _mod_norm(xe, npre_ref[...], mod_ref[0, 4:5, :], mod_ref[0, 3:4, :])
    rows = lax.broadcasted_iota(jnp.int32, (ext, D_MODEL), 0)
    valid = jnp.logical_and(jnp.logical_or(rows >= SUBLANES, t > 0),
                            jnp.logical_or(rows < tm + SUBLANES, t < pl.num_programs(1) - 1))
    h = jnp.where(valid, h, 0.0).astype(BF16)

    n_tiles = tm // SUBLANES
    sub = lax.broadcasted_iota(jnp.int32, (n_tiles, SUBLANES, FFN_COLS), 1)

    def conv_cols(lo):
        a = _dot(h, wu_ref[:, lo:lo + FFN_COLS]).reshape(n_tiles + 2, SUBLANES, FFN_COLS)
        down = pltpu.roll(a, 1, 1)
        up = pltpu.roll(a, SUBLANES - 1, 1)
        below = jnp.where(sub == 0, down[0:n_tiles], down[1:n_tiles + 1])
        above = jnp.where(sub == SUBLANES - 1, up[2:n_tiles + 2], up[1:n_tiles + 1])
        mid = a[1:n_tiles + 1]
        out = (below * cw_ref[0:1, lo:lo + FFN_COLS] + mid * cw_ref[1:2, lo:lo + FFN_COLS]
               + above * cw_ref[2:3, lo:lo + FFN_COLS] + cb_ref[:, lo:lo + FFN_COLS])
        return out.reshape(tm, FFN_COLS)

    for j in range(D_FF // FFN_COLS):
        gate = conv_cols(j * FFN_COLS)
        val = conv_cols(D_FF + j * FFN_COLS)
        act_scr[:, j * FFN_COLS:(j + 1) * FFN_COLS] = (jax.nn.gelu(gate, approximate=True) * val).astype(BF16)
    y = _dot(act_scr[...], wd_ref[...])
    o_ref[0] = x + mod_ref[0, 5:6, :] * (_rms(y) * npost_ref[...])


def _conv_ffn(x, mod, norm_pre, norm_post, ffn_up, conv_w, conv_b, ffn_down, tm):
    bn, seq, _ = x.shape
    prev_spec, next_spec = _halo_specs(tm, seq, D_MODEL)
    row = pl.BlockSpec((1, tm, D_MODEL), lambda b, t: (b, t, 0))
    return pl.pallas_call(
        _ffn_kernel,
        grid=(bn, seq // tm),
        in_specs=[row, prev_spec, next_spec, pl.BlockSpec((1, 6, D_MODEL), lambda b, t: (b, 0, 0)),
                  _const_spec((1, D_MODEL)), _const_spec((1, D_MODEL)),
                  _const_spec(ffn_up.shape), _const_spec(conv_w.shape), _const_spec((1, 2 * D_FF)),
                  _const_spec(ffn_down.shape)],
        out_specs=row,
        out_shape=jax.ShapeDtypeStruct(x.shape, F32),
        scratch_shapes=[pltpu.VMEM((tm, D_FF), BF16)],
        compiler_params=_cparams(2),
        name="conv_ffn",
    )(x, x, x, mod, norm_pre.reshape(1, D_MODEL), norm_post.reshape(1, D_MODEL), ffn_up, conv_w,
      conv_b.reshape(1, 2 * D_FF), ffn_down)


def _row_tile(seq):
    return min(512, seq)


def _encoder_layer(x, mod, lw, filters):
    seq = x.shape[1]
    tm = _row_tile(seq)
    p_fn, p_ret, p_hy, p_gla = _inproj(x, mod, lw["norm_pre_mix"], lw["w_mix"], min(INPROJ_ROWS, seq))

    o_fn = _fnet(p_fn)
    o_ret = _retention(p_ret, lw["ret_gn"], tm)

    spectrum, gsum = filters
    v, x1, x2 = _hyena_conv3(p_hy, lw["hy_conv_w"], lw["hy_conv_b"], tm)
    tn = min(4096, FFT_N2 * MIX_W)
    z = _long_conv_gate(v, x1, spectrum, 0, gsum[0], lw["hy_skip"][0], tn)
    o_hy = _long_conv_gate(z, x2, spectrum, 1, gsum[1], lw["hy_skip"][1], tn)

    o_gla = _gla(p_gla, lw["gla_w_decay"], lw["gla_b_decay"], lw["gla_gn"], tm)

    x = _merge(x, mod, lw["norm_pre_mix"], lw["norm_post_mix"], lw["w_gate"], lw["w_branch"], lw["w_out"],
               (o_fn, o_ret, o_hy, o_gla), tm)
    return _conv_ffn(x, mod, lw["norm_pre_ffn"], lw["norm_post_ffn"], lw["ffn_up"], lw["ffn_conv_w"],
                     lw["ffn_conv_b"], lw["ffn_down"], tm)


def kernel(x_prompt, x_sample, c_prompt, c_sample, ada_w, ada_b, norm_pre_mix, norm_post_mix, norm_pre_ffn, norm_post_ffn, w_in, hy_conv_w, hy_conv_b, flt_w1, flt_b1, flt_freq, flt_w2, flt_b2, flt_w3, flt_b3, hy_skip, gla_w_decay, gla_b_decay, ret_gn, gla_gn, w_branch, w_out, ffn_up, ffn_conv_w, ffn_conv_b, ffn_down):
    groups = [x_prompt, x_sample]
    n_rows = [c_prompt.shape[0], c_sample.shape[0]]
    c_rows = jnp.concatenate([c_prompt, c_sample], axis=0)
    pad = (-c_rows.shape[0]) % SUBLANES
    c_rows = jnp.pad(c_rows, ((0, pad), (0, 0)))
    for i in range(DEPTH):
        lw = {
            "norm_pre_mix": norm_pre_mix[i], "norm_post_mix": norm_post_mix[i],
            "norm_pre_ffn": norm_pre_ffn[i], "norm_post_ffn": norm_post_ffn[i],
            "w_mix": jnp.pad(w_in[i][:, :N_MIX_IN], ((0, 0), (0, 2048 + GLA_IN_W - N_MIX_IN))).astype(BF16),
            "w_gate": w_in[i][:, N_MIX_IN:].astype(BF16),
            "hy_conv_w": hy_conv_w[i], "hy_conv_b": hy_conv_b[i], "hy_skip": hy_skip[i],
            "gla_w_decay": gla_w_decay[i], "gla_b_decay": gla_b_decay[i],
            "ret_gn": ret_gn[i], "gla_gn": gla_gn[i],
            "w_branch": w_branch[i].astype(BF16), "w_out": w_out[i].astype(BF16),
            "ffn_up": ffn_up[i].astype(BF16), "ffn_conv_w": ffn_conv_w[i], "ffn_conv_b": ffn_conv_b[i],
            "ffn_down": ffn_down[i].astype(BF16),
        }
        mod_all = _modulation(c_rows, ada_w[i], ada_b[i])
        filters = {}
        start = 0
        for gi, x in enumerate(groups):
            seq = x.shape[1]
            if seq not in filters:
                filters[seq] = _hyena_filters(seq, flt_w1[i], flt_b1[i], flt_freq[i], flt_w2[i], flt_b2[i],
                                              flt_w3[i], flt_b3[i])
            mod = mod_all[start:start + n_rows[gi]].reshape(n_rows[gi], 6, D_MODEL)
            start += n_rows[gi]
            groups[gi] = _encoder_layer(x, mod, lw, filters[seq])
    return tuple(groups)
```

```python
import functools
import math

import numpy as np
import jax
import jax.numpy as jnp
from jax import lax
from jax.experimental import pallas as pl
from jax.experimental.pallas import tpu as pltpu

F32 = jnp.float32
BF16 = jnp.bfloat16
HIGHEST = lax.Precision.HIGHEST

D_MODEL = 1024
DEPTH = 2
N_BRANCH = 4
MIX_W = 256
N_HEADS = 4
HEAD_V = MIX_W // N_HEADS
RET_DK = 64
GLA_DK = 32
GLA_RANK = 16
GLA_TAU = 16.0
HY_ORDER = 2
FLT_BANDS = 16
FLT_EMB = 1 + 2 * FLT_BANDS
FLT_EMB_PAD = 40
FLT_HIDDEN = 64
D_FF = 2816
GLA_CHUNK = 64
RET_CHUNK = 128
SCAN_MAX_BATCH_BLOCK = 4
SCAN_BLOCK_ROWS = 1024
SCAN_GROUP_ITEMS = 8
ROPE_BASE = 10000.0
EPS = 1e-6
N_MIX_IN = 2848
GLA_IN_W = 896

V7X_VMEM_BYTES = 64 * 1024 * 1024
VMEM_LIMIT = 52 * 1024 * 1024
SUBLANES = 8
LANES = 128
FFT_N2 = 128
INPROJ_ROWS = 1024


def _cparams(n_axes):
    return pltpu.CompilerParams(dimension_semantics=("arbitrary",) * n_axes,
                                vmem_limit_bytes=VMEM_LIMIT)


def _const_spec(shape):
    zeros = (0,) * len(shape)
    return pl.BlockSpec(shape, lambda *_: zeros, pipeline_mode=pl.Buffered(1))


def _dot(a, b):
    return jnp.dot(a, b, preferred_element_type=F32)


def _dot_nt(a, b):
    return lax.dot_general(a, b, (((1,), (1,)), ((), ())), preferred_element_type=F32)


def _dot_tn(a, b):
    return lax.dot_general(a, b, (((0,), (0,)), ((), ())), preferred_element_type=F32)


def _rms(x):
    return x * lax.rsqrt(jnp.mean(x * x, axis=-1, keepdims=True) + EPS)


def _silu(x):
    return x * jax.nn.sigmoid(x)


def _mod_kernel(c_ref, w_ref, b_ref, o_ref):
    c = c_ref[...]
    o_ref[...] = jnp.dot(_silu(c), w_ref[...], precision=HIGHEST, preferred_element_type=F32) + b_ref[...]


def _modulation(c_rows, ada_w, ada_b):
    rows = c_rows.shape[0]
    n_out = ada_w.shape[1]
    tn = D_MODEL
    return pl.pallas_call(
        _mod_kernel,
        grid=(n_out // tn,),
        in_specs=[pl.BlockSpec((rows, D_MODEL), lambda j: (0, 0)),
                  pl.BlockSpec((D_MODEL, tn), lambda j: (0, j)),
                  pl.BlockSpec((1, tn), lambda j: (0, j))],
        out_specs=pl.BlockSpec((rows, tn), lambda j: (0, j)),
        out_shape=jax.ShapeDtypeStruct((rows, n_out), F32),
        compiler_params=_cparams(1),
        name="modulation",
    )(c_rows, ada_w, ada_b.reshape(1, n_out))


def _mod_norm(x, norm_w, scale, shift):
    return _rms(x) * norm_w * (1.0 + scale) + shift


_MIX_SPLITS = ((0, 256), (256, 1280), (1280, 2048), (2048, 2048 + GLA_IN_W))


def _inproj_kernel(x_ref, mod_ref, nw_ref, w_ref, ofn_ref, oret_ref, ohy_ref, ogla_ref):
    h = _mod_norm(x_ref[0], nw_ref[...], mod_ref[0, 1:2, :], mod_ref[0, 0:1, :]).astype(BF16)
    for o_ref, (lo, hi) in zip((ofn_ref, oret_ref, ohy_ref, ogla_ref), _MIX_SPLITS):
        o_ref[0] = _dot(h, w_ref[:, lo:hi])


def _inproj(x, mod, norm_w, w_mix, tm):
    bn, seq, _ = x.shape
    widths = [hi - lo for lo, hi in _MIX_SPLITS]
    return pl.pallas_call(
        _inproj_kernel,
        grid=(bn, seq // tm),
        in_specs=[pl.BlockSpec((1, tm, D_MODEL), lambda b, t: (b, t, 0)),
                  pl.BlockSpec((1, 6, D_MODEL), lambda b, t: (b, 0, 0)),
                  _const_spec((1, D_MODEL)),
                  _const_spec(w_mix.shape)],
        out_specs=[pl.BlockSpec((1, tm, w), lambda b, t: (b, t, 0)) for w in widths],
        out_shape=[jax.ShapeDtypeStruct((bn, seq, w), F32) for w in widths],
        compiler_params=_cparams(2),
        name="inproj",
    )(x, mod, norm_w.reshape(1, D_MODEL), w_mix)


def _halo_specs(tm, seq, width):
    per = tm // SUBLANES
    last = seq // SUBLANES - 1
    prev_spec = pl.BlockSpec((1, SUBLANES, width), lambda b, t: (b, jnp.maximum(t * per - 1, 0), 0))
    next_spec = pl.BlockSpec((1, SUBLANES, width), lambda b, t: (b, jnp.minimum((t + 1) * per, last), 0))
    return prev_spec, next_spec


def _hyconv_kernel(u_ref, up_ref, un_ref, w_ref, b_ref, v_ref, x1_ref, x2_ref):
    t = pl.program_id(1)
    u = u_ref[0]
    tm = u.shape[0]
    prev_row = jnp.where(t > 0, up_ref[0, SUBLANES - 1:SUBLANES, :], 0.0)
    next_row = jnp.where(t < pl.num_programs(1) - 1, un_ref[0, 0:1, :], 0.0)
    rows = lax.broadcasted_iota(jnp.int32, u.shape, 0)
    below = jnp.where(rows == 0, prev_row, pltpu.roll(u, 1, 0))
    above = jnp.where(rows == tm - 1, next_row, pltpu.roll(u, tm - 1, 0))
    y = below * w_ref[0:1, :] + u * w_ref[1:2, :] + above * w_ref[2:3, :] + b_ref[...]
    v_ref[0] = y[:, 0:MIX_W].astype(v_ref.dtype)
    x1_ref[0] = y[:, MIX_W:2 * MIX_W].astype(x1_ref.dtype)
    x2_ref[0] = y[:, 2 * MIX_W:3 * MIX_W].astype(x2_ref.dtype)


def _hyena_conv3(u, conv_w, conv_b, tm):
    bn, seq, width = u.shape
    prev_spec, next_spec = _halo_specs(tm, seq, width)
    out = jax.ShapeDtypeStruct((bn, seq, MIX_W), BF16)
    return pl.pallas_call(
        _hyconv_kernel,
        grid=(bn, seq // tm),
        in_specs=[pl.BlockSpec((1, tm, width), lambda b, t: (b, t, 0)), prev_spec, next_spec,
                  _const_spec((3, width)), _const_spec((1, width))],
        out_specs=[pl.BlockSpec((1, tm, MIX_W), lambda b, t: (b, t, 0))] * 3,
        out_shape=[out, out, out],
        compiler_params=_cparams(2),
        name="hyena_conv3",
    )(u, u, u, conv_w, conv_b.reshape(1, width))


def _dft_outer(n_total, n1, k_rows):
    del n_total
    k = np.arange(n1)[:, None]
    n = np.arange(k_rows)[None, :]
    ang = 2.0 * np.pi * ((k * n) % n1) / n1
    return jnp.asarray(np.concatenate([np.cos(ang), -np.sin(ang)], axis=0), dtype=F32)


def _idft_outer(n1, out_rows):
    n = np.arange(out_rows)[:, None]
    k = np.arange(n1)[None, :]
    ang = 2.0 * np.pi * ((k * n) % n1) / n1
    return jnp.asarray(np.concatenate([np.cos(ang), -np.sin(ang)], axis=1), dtype=F32)


def _dft_inner_tables(n_total, n1):
    n2 = np.arange(FFT_N2)
    ang = 2.0 * np.pi * ((n2[:, None] * n2[None, :]) % FFT_N2) / FFT_N2
    k1 = np.arange(n1)[:, None]
    tw = 2.0 * np.pi * ((k1 * n2[None, :]) % n_total) / n_total
    as32 = lambda a: jnp.asarray(a, dtype=F32)
    return as32(np.cos(ang)), as32(np.sin(ang)), as32(np.cos(tw)), as32(np.sin(tw))


def _lmul_kernel(f_ref, x_ref, o_ref):
    o_ref[0] = _dot(f_ref[...].astype(BF16), x_ref[0].astype(BF16)).astype(o_ref.dtype)


def _fft_outer(x_view, fmat, tn):
    bn, k_rows, width = x_view.shape
    r_rows = fmat.shape[0]
    return pl.pallas_call(
        _lmul_kernel,
        grid=(bn, width // tn),
        in_specs=[_const_spec(fmat.shape),
                  pl.BlockSpec((1, k_rows, tn), lambda b, j: (b, 0, j))],
        out_specs=pl.BlockSpec((1, r_rows, tn), lambda b, j: (b, 0, j)),
        out_shape=jax.ShapeDtypeStruct((bn, r_rows, width), BF16),
        compiler_params=_cparams(2),
        name="fft_outer",
    )(fmat, x_view)


def _build_inner_mats(fc_ref, fs_ref, twc_ref, tws_ref, m_scr, mt_scr, tk1):
    fc = fc_ref[...]
    fs = fs_ref[...]
    for i in range(tk1):
        tc = twc_ref[i:i + 1, :]
        ts = tws_ref[i:i + 1, :]
        ar = fc * tc - fs * ts
        ai = -(fc * ts + fs * tc)
        m_scr[i] = jnp.concatenate(
            [jnp.concatenate([ar, -ai], axis=1), jnp.concatenate([ai, ar], axis=1)], axis=0).astype(BF16)
        if mt_scr is not None:
            art = ar.T
            ait = ai.T
            mt_scr[i] = jnp.concatenate(
                [jnp.concatenate([art, ait], axis=1), jnp.concatenate([-ait, art], axis=1)], axis=0).astype(BF16)


def _inner_forward(m_scr, y_ref, i):
    ycat = jnp.concatenate([y_ref[0, 0, i], y_ref[0, 1, i]], axis=0)
    z = _dot(m_scr[i], ycat)
    return z[:FFT_N2], z[FFT_N2:]


def _spec_mid_kernel(tk1, fc_ref, fs_ref, twc_ref, tws_ref, yf_ref, yb_ref, o_ref, m_scr):
    _build_inner_mats(fc_ref, fs_ref, twc_ref, tws_ref, m_scr, None, tk1)
    for i in range(tk1):
        fr, fi = _inner_forward(m_scr, yf_ref, i)
        br, bi = _inner_forward(m_scr, yb_ref, i)
        o_ref[i, 0] = fr + br
        o_ref[i, 1] = fi - bi


def _conv_mid_kernel(tk1, fc_ref, fs_ref, twc_ref, tws_ref, y_ref, g_ref, o_ref, m_scr, mt_scr):
    @pl.when(pl.program_id(1) == 0)
    def _():
        _build_inner_mats(fc_ref, fs_ref, twc_ref, tws_ref, m_scr, mt_scr, tk1)

    for i in range(tk1):
        zr, zi = _inner_forward(m_scr, y_ref, i)
        gr = g_ref[i, 0]
        gi = g_ref[i, 1]
        pcat = jnp.concatenate([zr * gr - zi * gi, zr * gi + zi * gr], axis=0).astype(BF16)
        q = _dot(mt_scr[i], pcat)
        o_ref[0, 0, i] = q[:FFT_N2].astype(BF16)
        o_ref[0, 1, i] = q[FFT_N2:].astype(BF16)


def _fnet_mid_kernel(tk1, scale, fc_ref, fs_ref, twc_ref, tws_ref, y_ref, cs_ref, o_ref, m_scr):
    @pl.when(pl.program_id(1) == 0)
    def _():
        _build_inner_mats(fc_ref, fs_ref, twc_ref, tws_ref, m_scr, None, tk1)

    for i in range(tk1):
        zr, zi = _inner_forward(m_scr, y_ref, i)
        zcat = jnp.concatenate([zr, zi], axis=1).astype(BF16)
        o_ref[0, :, i * MIX_W:(i + 1) * MIX_W] = (_dot(zcat, cs_ref[...].astype(BF16)) * scale).astype(o_ref.dtype)


def _inner_table_specs(tk1):
    return [_const_spec((FFT_N2, FFT_N2)), _const_spec((FFT_N2, FFT_N2)),
            pl.BlockSpec((tk1, FFT_N2), lambda k, b: (k, 0)),
            pl.BlockSpec((tk1, FFT_N2), lambda k, b: (k, 0))]


FFT_TK1 = 16


def _filter_spectrum(g_fwd, g_bwd):
    seq, chans = g_fwd.shape
    n_total = 2 * seq
    n1 = n_total // FFT_N2
    half = n1 // 2
    fmat = _dft_outer(n_total, n1, half)
    outer = lambda g: _fft_outer(g.reshape(1, half, FFT_N2 * chans), fmat, 4096).reshape(
        1, 2, n1, FFT_N2, chans)
    tables = _dft_inner_tables(n_total, n1)
    tk1 = FFT_TK1
    y_spec = pl.BlockSpec((1, 2, tk1, FFT_N2, chans), lambda k, b: (0, 0, k, 0, 0))
    return pl.pallas_call(
        functools.partial(_spec_mid_kernel, tk1),
        grid=(n1 // tk1, 1),
        in_specs=_inner_table_specs(tk1) + [y_spec, y_spec],
        out_specs=pl.BlockSpec((tk1, 2, FFT_N2, chans), lambda k, b: (k, 0, 0, 0)),
        out_shape=jax.ShapeDtypeStruct((n1, 2, FFT_N2, chans), F32),
        scratch_shapes=[pltpu.VMEM((tk1, 2 * FFT_N2, 2 * FFT_N2), BF16)],
        compiler_params=_cparams(2),
        name="filter_spectrum",
    )(*tables, outer(g_fwd), outer(g_bwd))


def _hy_out_kernel(n_total, f_ref, q_ref, v_ref, x_ref, gs_ref, sk_ref, o_ref):
    y = _dot(f_ref[...].astype(BF16), q_ref[0])
    inv = 1.0 / ((gs_ref[...] + EPS) * n_total)
    o_ref[0] = (x_ref[0] * (y * inv + sk_ref[...] * v_ref[0])).astype(o_ref.dtype)


def _long_conv_gate(v, x_gate, spectrum, order, gsum, skip, tn):
    bn, seq, chans = v.shape
    n_total = 2 * seq
    n1 = n_total // FFT_N2
    half = n1 // 2
    width = FFT_N2 * chans
    v_view = v.reshape(bn, half, width)
    y = _fft_outer(v_view, _dft_outer(n_total, n1, half), tn).reshape(bn, 2, n1, FFT_N2, chans)
    tables = _dft_inner_tables(n_total, n1)
    tk1 = FFT_TK1
    q = pl.pallas_call(
        functools.partial(_conv_mid_kernel, tk1),
        grid=(n1 // tk1, bn),
        in_specs=_inner_table_specs(tk1) + [
            pl.BlockSpec((1, 2, tk1, FFT_N2, chans), lambda k, b: (b, 0, k, 0, 0)),
            pl.BlockSpec((tk1, 2, FFT_N2, chans), lambda k, b: (k, 0, 0, order))],
        out_specs=pl.BlockSpec((1, 2, tk1, FFT_N2, chans), lambda k, b: (b, 0, k, 0, 0)),
        out_shape=jax.ShapeDtypeStruct((bn, 2, n1, FFT_N2, chans), BF16),
        scratch_shapes=[pltpu.VMEM((tk1, 2 * FFT_N2, 2 * FFT_N2), BF16),
                        pltpu.VMEM((tk1, 2 * FFT_N2, 2 * FFT_N2), BF16)],
        compiler_params=_cparams(2),
        name="long_conv_mid",
    )(*tables, y, spectrum)
    q_view = q.reshape(bn, 2 * n1, width)
    reps = tn // chans
    gs_row = jnp.tile(gsum.reshape(1, chans), (1, reps))
    sk_row = jnp.tile(skip.reshape(1, chans), (1, reps))
    fmat = _idft_outer(n1, half)
    out = pl.pallas_call(
        functools.partial(_hy_out_kernel, float(n_total)),
        grid=(bn, width // tn),
        in_specs=[_const_spec(fmat.shape),
                  pl.BlockSpec((1, 2 * n1, tn), lambda b, j: (b, 0, j)),
                  pl.BlockSpec((1, half, tn), lambda b, j: (b, 0, j)),
                  pl.BlockSpec((1, half, tn), lambda b, j: (b, 0, j)),
                  _const_spec((1, tn)), _const_spec((1, tn))],
        out_specs=pl.BlockSpec((1, half, tn), lambda b, j: (b, 0, j)),
        out_shape=jax.ShapeDtypeStruct((bn, half, width), BF16),
        compiler_params=_cparams(2),
        name="long_conv_out",
    )(fmat, q_view, v_view, x_gate.reshape(bn, half, width), gs_row, sk_row)
    return out.reshape(bn, seq, chans)


def _filter_kernel(feat_ref, w1_ref, b1_ref, fq_ref, w2_ref, b2_ref, w3_ref, b3_ref, dl_ref,
                   gf_ref, gb_ref, s_ref):
    i = pl.program_id(0)
    f = feat_ref[...]
    tm = f.shape[0]
    fq = fq_ref[...]
    hp = functools.partial(jnp.dot, precision=HIGHEST, preferred_element_type=F32)
    h = jnp.sin(fq * (hp(f, w1_ref[...]) + b1_ref[...]))
    h = jnp.sin(fq * (hp(h, w2_ref[...]) + b2_ref[...]))
    filt = hp(h, w3_ref[...]) + b3_ref[...]
    win = jnp.exp(-f[:, 0:1] * dl_ref[...])
    lag = i * tm + lax.broadcasted_iota(jnp.int32, (tm, MIX_W), 0)
    sums = []
    for o in range(HY_ORDER):
        hf = filt[:, (2 * o) * MIX_W:(2 * o + 1) * MIX_W] * win
        hb = jnp.where(lag > 0, filt[:, (2 * o + 1) * MIX_W:(2 * o + 2) * MIX_W] * win, 0.0)
        gf_ref[:, o * MIX_W:(o + 1) * MIX_W] = hf
        gb_ref[:, o * MIX_W:(o + 1) * MIX_W] = hb
        sums.append(jnp.sum(jnp.abs(hf) + jnp.abs(hb), axis=0, keepdims=True))
    total = jnp.concatenate(sums, axis=1)

    @pl.when(i == 0)
    def _():
        s_ref[...] = total

    @pl.when(i > 0)
    def _():
        s_ref[...] = s_ref[...] + total


def _hyena_filter_features(seq):
    t = jnp.linspace(0.0, 1.0, seq, dtype=F32)[:, None]
    w = 2.0 * math.pi * jnp.arange(seq, dtype=F32)[:, None] / seq
    f = jnp.linspace(1e-4, FLT_BANDS - 1, FLT_BANDS, dtype=F32)[None, :]
    feat = jnp.concatenate([t, jnp.cos(f * w), -jnp.sin(f * w)], axis=-1)
    return jnp.pad(feat, ((0, 0), (0, FLT_EMB_PAD - FLT_EMB)))


def _hyena_filters(seq, flt_w1, flt_b1, flt_freq, flt_w2, flt_b2, flt_w3, flt_b3):
    tm = 1024
    feat = _hyena_filter_features(seq)
    deltas = jnp.abs(jnp.linspace(math.log(1e-2) / 0.3, math.log(1e-2) / 1.5, MIX_W, dtype=F32))
    w1 = jnp.pad(flt_w1, ((0, FLT_EMB_PAD - FLT_EMB), (0, 0)))
    n_out = HY_ORDER * 2 * MIX_W
    taps = jax.ShapeDtypeStruct((seq, HY_ORDER * MIX_W), F32)
    g_fwd, g_bwd, gsum = pl.pallas_call(
        _filter_kernel,
        grid=(seq // tm,),
        in_specs=[pl.BlockSpec((tm, FLT_EMB_PAD), lambda i: (i, 0)),
                  _const_spec((FLT_EMB_PAD, FLT_HIDDEN)), _const_spec((1, FLT_HIDDEN)),
                  _const_spec((1, FLT_HIDDEN)),
                  _const_spec((FLT_HIDDEN, FLT_HIDDEN)), _const_spec((1, FLT_HIDDEN)),
                  _const_spec((FLT_HIDDEN, n_out)), _const_spec((1, n_out)),
                  _const_spec((1, MIX_W))],
        out_specs=[pl.BlockSpec((tm, HY_ORDER * MIX_W), lambda i: (i, 0)),
                   pl.BlockSpec((tm, HY_ORDER * MIX_W), lambda i: (i, 0)),
                   pl.BlockSpec((1, HY_ORDER * MIX_W), lambda i: (0, 0))],
        out_shape=[taps, taps, jax.ShapeDtypeStruct((1, HY_ORDER * MIX_W), F32)],
        compiler_params=_cparams(1),
        name="hyena_filter",
    )(feat, w1, flt_b1.reshape(1, -1), flt_freq.reshape(1, -1), flt_w2, flt_b2.reshape(1, -1),
      flt_w3, flt_b3.reshape(1, -1), deltas.reshape(1, MIX_W))
    return _filter_spectrum(g_fwd, g_bwd), gsum.reshape(HY_ORDER, MIX_W)


def _fnet_channel_mats():
    c = np.arange(MIX_W)
    same = (c[:, None] // HEAD_V) == (c[None, :] // HEAD_V)
    ang = 2.0 * np.pi * (((c[:, None] % HEAD_V) * (c[None, :] % HEAD_V)) % HEAD_V) / HEAD_V
    cs = np.concatenate([np.where(same, np.cos(ang), 0.0), np.where(same, np.sin(ang), 0.0)], axis=0)
    return jnp.asarray(cs, dtype=F32)


def _fnet(u):
    bn, seq, chans = u.shape
    n1 = seq // FFT_N2
    width = FFT_N2 * chans
    y = _fft_outer(u.reshape(bn, n1, width), _dft_outer(seq, n1, n1), 4096)
    y = y.reshape(bn, 2, n1, FFT_N2, chans)
    tables = _dft_inner_tables(seq, n1)
    tk1 = FFT_TK1
    scale = 1.0 / math.sqrt(seq * HEAD_V)
    out = pl.pallas_call(
        functools.partial(_fnet_mid_kernel, tk1, scale),
        grid=(n1 // tk1, bn),
        in_specs=_inner_table_specs(tk1) + [
            pl.BlockSpec((1, 2, tk1, FFT_N2, chans), lambda k, b: (b, 0, k, 0, 0)),
            _const_spec((2 * chans, chans))],
        out_specs=pl.BlockSpec((1, FFT_N2, tk1 * chans), lambda k, b: (b, 0, k)),
        out_shape=jax.ShapeDtypeStruct((bn, FFT_N2, n1 * chans), BF16),
        scratch_shapes=[pltpu.VMEM((tk1, 2 * FFT_N2, 2 * FFT_N2), BF16)],
        compiler_params=_cparams(2),
        name="fnet_mid",
    )(*tables, y, _fnet_channel_mats())
    return out.reshape(bn, seq, chans)


def _split3(x):
    a = x.astype(BF16)
    r = x - a.astype(F32)
    b = r.astype(BF16)
    c = (r - b.astype(F32)).astype(BF16)
    return a, b, c


def _group_mean_mat():
    c = np.arange(MIX_W)
    same = (c[:, None] // HEAD_V) == (c[None, :] // HEAD_V)
    return jnp.asarray(np.where(same, 1.0 / HEAD_V, 0.0), dtype=BF16)


def _group_mean(x, avg):
    hi = x.astype(BF16)
    lo = (x - hi.astype(F32)).astype(BF16)
    return _dot(hi, avg) + _dot(lo, avg)


def _scan_kernel(cfg, *refs):
    chunk, dk_tot, reverse, is_ret, final = cfg
    refs = list(refs)
    q_ref, k_ref, v_ref = refs[:3]
    pos = 3
    if is_ret:
        cos_ref, sin_ref, lg_ref = refs[pos:pos + 3]
        pos += 3
    else:
        lr_ref, wd_ref, bd_ref = refs[pos:pos + 3]
        pos += 3
    if final:
        ofwd_ref, gate_ref, gain_ref, avg_ref = refs[pos:pos + 4]
        pos += 4
    o_ref, st_ref = refs[pos], refs[pos + 1]

    @pl.when(pl.program_id(1) == 0)
    def _():
        st_ref[...] = jnp.zeros_like(st_ref)

    dk = dk_tot // N_HEADS
    tb = q_ref.shape[1]
    n_chunks = tb // chunk
    wide = N_HEADS * chunk

    def iota(shape, axis):
        return lax.broadcasted_iota(jnp.int32, shape, axis)

    head_k = (iota((wide, dk_tot), 0) // chunk) == (iota((wide, dk_tot), 1) // dk)
    head_v = (iota((wide, MIX_W), 0) // chunk) == (iota((wide, MIX_W), 1) // HEAD_V)
    head_s = (iota((MIX_W, dk_tot), 0) // HEAD_V) == (iota((MIX_W, dk_tot), 1) // dk)
    ri = iota((chunk, wide), 0)
    ci = iota((chunk, wide), 1) % chunk
    keep = (ci > ri) if reverse else (ci <= ri)
    if not is_ret:
        ti = iota((chunk, chunk), 0)
        tj = iota((chunk, chunk), 1)
        tri = jnp.where((tj >= ti) if reverse else (tj <= ti), 1.0, 0.0).astype(BF16)
    if is_ret:
        lane = iota((chunk, MIX_W), 1)
        first_half = (lane % RET_DK) < (RET_DK // 2)
        steps = iota((chunk, dk_tot), 0)
        steps = ((chunk - steps) if reverse else (steps + 1)).astype(F32)

    if is_ret:
        lg = lg_ref[...]
        b_ret = steps * lg
        edge_ret = float(chunk) * lg
        b_mid_ret = float(chunk // 2 + 1) * lg

    bb = q_ref.shape[0]
    group_chunks = max(1, SCAN_GROUP_ITEMS // bb)

    def rotary(x, cos, sin):
        swapped = jnp.where(first_half, pltpu.roll(x, MIX_W - RET_DK // 2, 1),
                            pltpu.roll(x, RET_DK // 2, 1))
        return x * cos + swapped * sin

    def run_group(chunks, states):
        items = [(bi, slice(c * chunk, (c + 1) * chunk)) for c in chunks for bi in range(bb)]
        n = len(items)
        qs_in = [q_ref[bi, rows, :] for bi, rows in items]
        ks_in = [k_ref[bi, rows, :] for bi, rows in items]
        vs = [v_ref[bi, rows, :] for bi, rows in items]
        if is_ret:
            qs_in = [rotary(q, cos_ref[rows, :], sin_ref[rows, :]) for q, (_, rows) in zip(qs_in, items)]
            ks_in = [rotary(k, cos_ref[rows, :], sin_ref[rows, :]) for k, (_, rows) in zip(ks_in, items)]
            bs, edges, mids = [b_ret] * n, [edge_ret] * n, [b_mid_ret] * n
        else:
            lr = jnp.concatenate([lr_ref[bi, rows, :] for bi, rows in items], axis=0)
            z = jnp.dot(lr, wd_ref[...], precision=HIGHEST, preferred_element_type=F32) + bd_ref[...]
            g = (jnp.minimum(z, 0.0) - jnp.log1p(jnp.exp(-jnp.abs(z)))) * (1.0 / GLA_TAU)
            parts = []
            for i in range(n):
                parts.extend(_split3(g[i * chunk:(i + 1) * chunk]))
            cum = _dot(tri, jnp.concatenate(parts, axis=1))
            bs = [cum[:, (3 * i) * dk_tot:(3 * i + 1) * dk_tot] + cum[:, (3 * i + 1) * dk_tot:(3 * i + 2) * dk_tot]
                  + cum[:, (3 * i + 2) * dk_tot:(3 * i + 3) * dk_tot] for i in range(n)]
            edges = [b[0:1, :] if reverse else b[chunk - 1:chunk, :] for b in bs]
            mids = [b[chunk // 2:chunk // 2 + 1, :] for b in bs]
        qs_in = [q * (dk ** -0.5) for q in qs_in]
        q_intra = [(q * jnp.exp(b - m)).astype(BF16) for q, b, m in zip(qs_in, bs, mids)]
        k_intra = [k * jnp.exp(m - b) for k, b, m in zip(ks_in, bs, mids)]
        q_inter = [(q * jnp.exp(b)).astype(BF16) for q, b in zip(qs_in, bs)]
        k_state = [(k * jnp.exp(e - b)).astype(BF16) for k, b, e in zip(ks_in, bs, edges)]
        k_bd = [jnp.where(head_k, jnp.concatenate([k] * N_HEADS, axis=0), 0.0).astype(BF16) for k in k_intra]
        v_bd = [jnp.where(head_v, jnp.concatenate([v] * N_HEADS, axis=0), 0.0).astype(BF16) for v in vs]
        scores = [_dot_nt(q, k) for q, k in zip(q_intra, k_bd)]
        scores = [jnp.where(keep, s, 0.0).astype(BF16) for s in scores]
        intra = [_dot(s, v) for s, v in zip(scores, v_bd)]
        updates = [jnp.where(head_s, _dot_tn(v.astype(BF16), k), 0.0) for v, k in zip(vs, k_state)]
        decays = [jnp.exp(e) for e in edges]
        outs = []
        for i, (bi, _) in enumerate(items):
            outs.append(intra[i] + _dot_nt(q_inter[i], states[bi].astype(BF16)))
            states[bi] = states[bi] * decays[i] + updates[i]
        if final:
            o = jnp.concatenate([o + ofwd_ref[bi, rows, :] for o, (bi, rows) in zip(outs, items)], axis=0)
            gate = jnp.concatenate([gate_ref[bi, rows, :] for bi, rows in items], axis=0)
            avg = avg_ref[...]
            if is_ret:
                o = o - _group_mean(o, avg)
            o = o * lax.rsqrt(_group_mean(o * o, avg) + EPS)
            o = o * gain_ref[...] * _silu(gate)
            outs = [o[i * chunk:(i + 1) * chunk] for i in range(n)]
        for o, (bi, rows) in zip(outs, items):
            o_ref[bi, rows, :] = o.astype(o_ref.dtype)

    states = [st_ref[bi] for bi in range(bb)]
    order = list(range(n_chunks - 1, -1, -1) if reverse else range(n_chunks))
    for g0 in range(0, n_chunks, group_chunks):
        run_group(order[g0:g0 + group_chunks], states)
    for bi in range(bb):
        st_ref[bi] = states[bi]


def _scan_call(src, cols, extra, extra_specs, seq_block, chunk, dk_tot, reverse, is_ret, final_args):
    bn, seq, _ = src.shape
    bb = bn if bn <= SCAN_MAX_BATCH_BLOCK else SCAN_MAX_BATCH_BLOCK
    seq_block = min(seq, SCAN_BLOCK_ROWS // bb)
    nb = seq // seq_block
    blk = (lambda t: nb - 1 - t) if reverse else (lambda t: t)
    row_spec = lambda w, col: pl.BlockSpec((bb, seq_block, w), lambda b, t: (b, blk(t), col))
    in_specs = [row_spec(dk_tot, cols[0]), row_spec(dk_tot, cols[1]), row_spec(MIX_W, cols[2])]
    args = [src, src, src]
    for a, s in zip(extra, extra_specs):
        args.append(a)
        in_specs.append(s(blk, bb, seq_block) if callable(s) else s)
    final = final_args is not None
    if final:
        o_fwd, gate_col, gain = final_args
        args += [o_fwd, src, gain.reshape(1, MIX_W), _group_mean_mat()]
        in_specs += [row_spec(MIX_W, 0), row_spec(MIX_W, gate_col), _const_spec((1, MIX_W)),
                     _const_spec((MIX_W, MIX_W))]
    cfg = (chunk, dk_tot, reverse, is_ret, final)
    return pl.pallas_call(
        functools.partial(_scan_kernel, cfg),
        grid=(bn // bb, nb),
        in_specs=in_specs,
        out_specs=pl.BlockSpec((bb, seq_block, MIX_W), lambda b, t: (b, blk(t), 0)),
        out_shape=jax.ShapeDtypeStruct((bn, seq, MIX_W), BF16 if final else F32),
        scratch_shapes=[pltpu.VMEM((bb, MIX_W, dk_tot), F32)],
        compiler_params=_cparams(2),
        name=("ret" if is_ret else "gla") + ("_bwd" if reverse else "_fwd"),
    )(*args)


def _rotary_tables(seq):
    half = RET_DK // 2
    inv = ROPE_BASE ** (-jnp.arange(half, dtype=F32) / half)
    ang = jnp.arange(seq, dtype=F32)[:, None] * inv[None, :]
    cos = jnp.cos(ang)
    sin = jnp.sin(ang)
    cos_t = jnp.tile(jnp.concatenate([cos, cos], axis=1), (1, N_HEADS))
    sin_t = jnp.tile(jnp.concatenate([-sin, sin], axis=1), (1, N_HEADS))
    return cos_t, sin_t


def _retention(p_ret, ret_gn, seq_block):
    bn, seq, _ = p_ret.shape
    cos_t, sin_t = _rotary_tables(seq)
    log_gamma = jnp.log(1.0 - 2.0 ** (-5.0 - jnp.arange(N_HEADS, dtype=F32)))
    lg = jnp.repeat(log_gamma, RET_DK).reshape(1, MIX_W)
    extra = [cos_t, sin_t, lg]
    tab = lambda blk, bb, rows: pl.BlockSpec((rows, MIX_W), lambda b, t: (blk(t), 0))
    specs = [tab, tab, _const_spec((1, MIX_W))]
    common = dict(src=p_ret, cols=(0, 1, 2), extra=extra, extra_specs=specs, seq_block=seq_block,
                  chunk=RET_CHUNK, dk_tot=N_HEADS * RET_DK, is_ret=True)
    o_fwd = _scan_call(reverse=False, final_args=None, **common)
    return _scan_call(reverse=True, final_args=(o_fwd, 3, ret_gn), **common)


def _gla(p_gla, w_decay, b_decay, gla_gn, seq_block):
    dk_tot = N_HEADS * GLA_DK
    outs = []
    o_fwd = None
    for direction in range(2):
        wd = jnp.zeros((LANES, dk_tot), F32).at[direction * GLA_RANK:(direction + 1) * GLA_RANK].set(
            w_decay[direction])
        extra = [p_gla, wd, b_decay[direction].reshape(1, dk_tot)]
        lr_spec = lambda blk, bb, rows: pl.BlockSpec((bb, rows, LANES), lambda b, t: (b, blk(t), 6))
        specs = [lr_spec, _const_spec((LANES, dk_tot)), _const_spec((1, dk_tot))]
        final_args = None if direction == 0 else (o_fwd, 2, gla_gn)
        out = _scan_call(src=p_gla, cols=(0, 1, 1), extra=extra, extra_specs=specs, seq_block=seq_block,
                         chunk=GLA_CHUNK, dk_tot=dk_tot, reverse=bool(direction), is_ret=False,
                         final_args=final_args)
        o_fwd = out
        outs.append(out)
    return outs[1]


def _merge_kernel(x_ref, mod_ref, npre_ref, npost_ref, wg_ref, wb_ref, wo_ref,
                  fn_ref, ret_ref, hy_ref, gla_ref, o_ref):
    x = x_ref[0]
    h = _mod_norm(x, npre_ref[...], mod_ref[0, 1:2, :], mod_ref[0, 0:1, :]).astype(BF16)
    merged = None
    for n, br in enumerate((fn_ref, ret_ref, hy_ref, gla_ref)):
        gate = jax.nn.sigmoid(_dot(h, wg_ref[:, n * D_MODEL:(n + 1) * D_MODEL]))
        term = gate * _dot(br[0].astype(BF16), wb_ref[n])
        merged = term if merged is None else merged + term
    y = _dot(merged.astype(BF16), wo_ref[...])
    o_ref[0] = x + mod_ref[0, 2:3, :] * (_rms(y) * npost_ref[...])


def _merge(x, mod, norm_pre, norm_post, w_gate, w_branch, w_out, branches, tm):
    bn, seq, _ = x.shape
    row = lambda w: pl.BlockSpec((1, tm, w), lambda b, t: (b, t, 0))
    return pl.pallas_call(
        _merge_kernel,
        grid=(bn, seq // tm),
        in_specs=[row(D_MODEL), pl.BlockSpec((1, 6, D_MODEL), lambda b, t: (b, 0, 0)),
                  _const_spec((1, D_MODEL)), _const_spec((1, D_MODEL)),
                  _const_spec(w_gate.shape), _const_spec(w_branch.shape), _const_spec(w_out.shape)]
                 + [row(MIX_W)] * N_BRANCH,
        out_specs=row(D_MODEL),
        out_shape=jax.ShapeDtypeStruct(x.shape, F32),
        compiler_params=_cparams(2),
        name="merge",
    )(x, mod, norm_pre.reshape(1, D_MODEL), norm_post.reshape(1, D_MODEL), w_gate, w_branch, w_out,
      *branches)


FFN_COLS = 256


def _ffn_kernel(x_ref, xp_ref, xn_ref, mod_ref, npre_ref, npost_ref, wu_ref, cw_ref, cb_ref, wd_ref, o_ref,
                act_scr):
    t = pl.program_id(1)
    x = x_ref[0]
    tm = x.shape[0]
    ext = tm + 2 * SUBLANES
    xe = jnp.concatenate([xp_ref[0], x, xn_ref[0]], axis=0)
    h = _mod_norm(xe, npre_ref[...], mod_ref[0, 4:5, :], mod_ref[0, 3:4, :])
    rows = lax.broadcasted_iota(jnp.int32, (ext, D_MODEL), 0)
    valid = jnp.logical_and(jnp.logical_or(rows >= SUBLANES, t > 0),
                            jnp.logical_or(rows < tm + SUBLANES, t < pl.num_programs(1) - 1))
    h = jnp.where(valid, h, 0.0).astype(BF16)

    n_tiles = tm // SUBLANES
    sub = lax.broadcasted_iota(jnp.int32, (n_tiles, SUBLANES, FFN_COLS), 1)

    def conv_cols(lo):
        a = _dot(h, wu_ref[:, lo:lo + FFN_COLS]).reshape(n_tiles + 2, SUBLANES, FFN_COLS)
        down = pltpu.roll(a, 1, 1)
        up = pltpu.roll(a, SUBLANES - 1, 1)
        below = jnp.where(sub == 0, down[0:n_tiles], down[1:n_tiles + 1])
        above = jnp.where(sub == SUBLANES - 1, up[2:n_tiles + 2], up[1:n_tiles + 1])
        mid = a[1:n_tiles + 1]
        out = (below * cw_ref[0:1, lo:lo + FFN_COLS] + mid * cw_ref[1:2, lo:lo + FFN_COLS]
               + above * cw_ref[2:3, lo:lo + FFN_COLS] + cb_ref[:, lo:lo + FFN_COLS])
        return out.reshape(tm, FFN_COLS)

    for j in range(D_FF // FFN_COLS):
        gate = conv_cols(j * FFN_COLS)
        val = conv_cols(D_FF + j * FFN_COLS)
        act_scr[:, j * FFN_COLS:(j + 1) * FFN_COLS] = (jax.nn.gelu(gate, approximate=True) * val).astype(BF16)
    y = _dot(act_scr[...], wd_ref[...])
    o_ref[0] = x + mod_ref[0, 5:6, :] * (_rms(y) * npost_ref[...])


def _conv_ffn(x, mod, norm_pre, norm_post, ffn_up, conv_w, conv_b, ffn_down, tm):
    bn, seq, _ = x.shape
    prev_spec, next_spec = _halo_specs(tm, seq, D_MODEL)
    row = pl.BlockSpec((1, tm, D_MODEL), lambda b, t: (b, t, 0))
    return pl.pallas_call(
        _ffn_kernel,
        grid=(bn, seq // tm),
        in_specs=[row, prev_spec, next_spec, pl.BlockSpec((1, 6, D_MODEL), lambda b, t: (b, 0, 0)),
                  _const_spec((1, D_MODEL)), _const_spec((1, D_MODEL)),
                  _const_spec(ffn_up.shape), _const_spec(conv_w.shape), _const_spec((1, 2 * D_FF)),
                  _const_spec(ffn_down.shape)],
        out_specs=row,
        out_shape=jax.ShapeDtypeStruct(x.shape, F32),
        scratch_shapes=[pltpu.VMEM((tm, D_FF), BF16)],
        compiler_params=_cparams(2),
        name="conv_ffn",
    )(x, x, x, mod, norm_pre.reshape(1, D_MODEL), norm_post.reshape(1, D_MODEL), ffn_up, conv_w,
      conv_b.reshape(1, 2 * D_FF), ffn_down)


def _row_tile(seq):
    return min(512, seq)


def _encoder_layer(x, mod, lw, filters):
    seq = x.shape[1]
    tm = _row_tile(seq)
    p_fn, p_ret, p_hy, p_gla = _inproj(x, mod, lw["norm_pre_mix"], lw["w_mix"], min(INPROJ_ROWS, seq))

    o_fn = _fnet(p_fn)
    o_ret = _retention(p_ret, lw["ret_gn"], tm)

    spectrum, gsum = filters
    v, x1, x2 = _hyena_conv3(p_hy, lw["hy_conv_w"], lw["hy_conv_b"], tm)
    tn = min(4096, FFT_N2 * MIX_W)
    z = _long_conv_gate(v, x1, spectrum, 0, gsum[0], lw["hy_skip"][0], tn)
    o_hy = _long_conv_gate(z, x2, spectrum, 1, gsum[1], lw["hy_skip"][1], tn)

    o_gla = _gla(p_gla, lw["gla_w_decay"], lw["gla_b_decay"], lw["gla_gn"], tm)

    x = _merge(x, mod, lw["norm_pre_mix"], lw["norm_post_mix"], lw["w_gate"], lw["w_branch"], lw["w_out"],
               (o_fn, o_ret, o_hy, o_gla), tm)
    return _conv_ffn(x, mod, lw["norm_pre_ffn"], lw["norm_post_ffn"], lw["ffn_up"], lw["ffn_conv_w"],
                     lw["ffn_conv_b"], lw["ffn_down"], tm)


def kernel(x_prompt, x_sample, c_prompt, c_sample, ada_w, ada_b, norm_pre_mix, norm_post_mix, norm_pre_ffn, norm_post_ffn, w_in, hy_conv_w, hy_conv_b, flt_w1, flt_b1, flt_freq, flt_w2, flt_b2, flt_w3, flt_b3, hy_skip, gla_w_decay, gla_b_decay, ret_gn, gla_gn, w_branch, w_out, ffn_up, ffn_conv_w, ffn_conv_b, ffn_down):
    groups = [x_prompt, x_sample]
    n_rows = [c_prompt.shape[0], c_sample.shape[0]]
    c_rows = jnp.concatenate([c_prompt, c_sample], axis=0)
    pad = (-c_rows.shape[0]) % SUBLANES
    c_rows = jnp.pad(c_rows, ((0, pad), (0, 0)))
    for i in range(DEPTH):
        lw = {
            "norm_pre_mix": norm_pre_mix[i], "norm_post_mix": norm_post_mix[i],
            "norm_pre_ffn": norm_pre_ffn[i], "norm_post_ffn": norm_post_ffn[i],
            "w_mix": jnp.pad(w_in[i][:, :N_MIX_IN], ((0, 0), (0, 2048 + GLA_IN_W - N_MIX_IN))).astype(BF16),
            "w_gate": w_in[i][:, N_MIX_IN:].astype(BF16),
            "hy_conv_w": hy_conv_w[i], "hy_conv_b": hy_conv_b[i], "hy_skip": hy_skip[i],
            "gla_w_decay": gla_w_decay[i], "gla_b_decay": gla_b_decay[i],
            "ret_gn": ret_gn[i], "gla_gn": gla_gn[i],
            "w_branch": w_branch[i].astype(BF16), "w_out": w_out[i].astype(BF16),
            "ffn_up": ffn_up[i].astype(BF16), "ffn_conv_w": ffn_conv_w[i], "ffn_conv_b": ffn_conv_b[i],
            "ffn_down": ffn_down[i].astype(BF16),
        }
        mod_all = _modulation(c_rows, ada_w[i], ada_b[i])
        filters = {}
        start = 0
        for gi, x in enumerate(groups):
            seq = x.shape[1]
            if seq not in filters:
                filters[seq] = _hyena_filters(seq, flt_w1[i], flt_b1[i], flt_freq[i], flt_w2[i], flt_b2[i],
                                              flt_w3[i], flt_b3[i])
            mod = mod_all[start:start + n_rows[gi]].reshape(n_rows[gi], 6, D_MODEL)
            start += n_rows[gi]
            groups[gi] = _encoder_layer(x, mod, lw, filters[seq])
    return tuple(groups)
```
